```python
import jax, jax.numpy as jnp
from jax import lax
import numpy as np

D_MODEL = 1024
BATCH = 2
SEQ = 8192
DEPTH = 2
DEC_BATCH = 128
DEC_SEQ = 8
PAST_LEN = 16384
PAGE_SIZE = 128

MLA_HEADS = 6
MLA_NOPE = 64
MLA_ROPE = 32
MLA_V = 64
MLA_Q_LORA = 256
MLA_KV_LORA = 128
ROPE_THETA = 10000.0
MLA_SCALE = (MLA_NOPE + MLA_ROPE) ** -0.5
FOX_HEADS = 4
FOX_DH = 64
FOX_SCALE = FOX_DH ** -0.5
FOX_F_BIAS_INIT = 2.0
MLSTM_HEADS = 6
MLSTM_DK = 32
MLSTM_DV = 64
MLSTM_CHUNK = 128
MLSTM_F_BIAS_INIT = 3.0
MLA_WIDTH = MLA_HEADS * MLA_V
FOX_WIDTH = FOX_HEADS * FOX_DH
MLSTM_WIDTH = MLSTM_HEADS * MLSTM_DV
MIX_WIDTH = MLA_WIDTH + FOX_WIDTH + MLSTM_WIDTH
IN_SPLITS = (MLA_Q_LORA, MLA_KV_LORA, MLA_ROPE, FOX_WIDTH, FOX_WIDTH, FOX_WIDTH, FOX_HEADS,
             MLSTM_HEADS * MLSTM_DK, MLSTM_HEADS * MLSTM_DK, MLSTM_WIDTH, MLSTM_HEADS, MLSTM_HEADS, MLSTM_WIDTH)
D_IN = sum(IN_SPLITS)
D_FF = 2816
CONV_W = 3
PLE_DIM = 256
Q_BLOCK = 128
LN_EPS = 1e-5
RMS_EPS = 1e-6
ALPHA = (2 * DEPTH) ** 0.25
BETA = (8 * DEPTH) ** -0.25

kernel_name = 'hybrid_mla_fox_mlstm_decoder_step'


def layer_norm(x, g, b):
    xf = x.astype(jnp.float32)
    mu = jnp.mean(xf, axis=-1, keepdims=True)
    var = jnp.mean(jnp.square(xf - mu), axis=-1, keepdims=True)
    return ((xf - mu) * lax.rsqrt(var + LN_EPS) * g + b).astype(x.dtype)


def rms_norm(x, g):
    xf = x.astype(jnp.float32)
    return (xf * lax.rsqrt(jnp.mean(jnp.square(xf), axis=-1, keepdims=True) + RMS_EPS) * g).astype(x.dtype)


def rope_angles(pos):
    inv_freq = 1.0 / (ROPE_THETA ** (jnp.arange(0, MLA_ROPE, 2, dtype=jnp.float32) / MLA_ROPE))
    ang = pos.astype(jnp.float32)[:, None] * inv_freq[None, :]
    return jnp.cos(ang), jnp.sin(ang)


def apply_rope(x, cos, sin):
    half = x.shape[-1] // 2
    x1 = x[..., :half].astype(jnp.float32)
    x2 = x[..., half:].astype(jnp.float32)
    return jnp.concatenate([x1 * cos - x2 * sin, x1 * sin + x2 * cos], axis=-1).astype(x.dtype)


def joint_softmax(s_past, s_new):
    n_past = s_past.shape[-1]
    p = jax.nn.softmax(jnp.concatenate([s_past, s_new], axis=-1), axis=-1)
    return p[..., :n_past], p[..., n_past:]


def mla_prompt(q_abs, q_pe, lat, k_pe):
    B, H, S, C = q_abs.shape
    nb = S // Q_BLOCK
    qa = q_abs.reshape(B, H, nb, Q_BLOCK, C).transpose(2, 0, 1, 3, 4)
    qr = q_pe.reshape(B, H, nb, Q_BLOCK, MLA_ROPE).transpose(2, 0, 1, 3, 4)
    qpos = jnp.arange(S).reshape(nb, Q_BLOCK)
    kpos = jnp.arange(S)

    def block(args):
        a, r, tq = args
        s = (jnp.einsum('bhqc,bkc->bhqk', a, lat) + jnp.einsum('bhqr,bkr->bhqk', r, k_pe)).astype(jnp.float32) * MLA_SCALE
        s = jnp.where(kpos[None, :] <= tq[:, None], s, -jnp.inf)
        p = jax.nn.softmax(s, axis=-1).astype(lat.dtype)
        return jnp.einsum('bhqk,bkc->bhqc', p, lat)

    o = lax.map(block, (qa, qr, qpos))
    return o.transpose(1, 2, 0, 3, 4).reshape(B, H, S, C)


def mla_sample(q_abs, q_pe, lat, k_pe, lat_past, kpe_past):
    T = q_abs.shape[2]
    s_past = (jnp.einsum('bhtc,bsc->bhts', q_abs, lat_past)
              + jnp.einsum('bhtr,bsr->bhts', q_pe, kpe_past)).astype(jnp.float32) * MLA_SCALE
    s_new = (jnp.einsum('bhtc,bsc->bhts', q_abs, lat)
             + jnp.einsum('bhtr,bsr->bhts', q_pe, k_pe)).astype(jnp.float32) * MLA_SCALE
    s_new = jnp.where(jnp.tril(jnp.ones((T, T), dtype=bool)), s_new, -jnp.inf)
    p_past, p_new = joint_softmax(s_past, s_new)
    dt = lat.dtype
    return (jnp.einsum('bhts,bsc->bhtc', p_past.astype(dt), lat_past)
            + jnp.einsum('bhts,bsc->bhtc', p_new.astype(dt), lat))


def fox_prompt(q, k, v, logf):
    B, S, H, Dh = q.shape
    nb = S // Q_BLOCK
    F = jnp.cumsum(logf.astype(jnp.float32), axis=1).transpose(0, 2, 1)
    kh = k.transpose(0, 2, 1, 3)
    vh = v.transpose(0, 2, 1, 3)
    qb = q.transpose(0, 2, 1, 3).reshape(B, H, nb, Q_BLOCK, Dh).transpose(2, 0, 1, 3, 4)
    fb = F.reshape(B, H, nb, Q_BLOCK).transpose(2, 0, 1, 3)
    qpos = jnp.arange(S).reshape(nb, Q_BLOCK)
    kpos = jnp.arange(S)

    def block(args):
        qi, fi, tq = args
        s = (jnp.einsum('bhqd,bhkd->bhqk', qi, kh).astype(jnp.float32) * FOX_SCALE
             + fi[..., :, None] - F[:, :, None, :])
        s = jnp.where(kpos[None, :] <= tq[:, None], s, -jnp.inf)
        p = jax.nn.softmax(s, axis=-1).astype(vh.dtype)
        return jnp.einsum('bhqk,bhkd->bhqd', p, vh)

    o = lax.map(block, (qb, fb, qpos))
    return o.transpose(1, 0, 3, 2, 4).reshape(B, S, H, Dh)


def fox_sample(q, k, v, logf, k_past, v_past, logf_past):
    T = q.shape[1]
    lf_p = logf_past.astype(jnp.float32)
    suffix = (lax.cumsum(lf_p, axis=1, reverse=True) - lf_p).transpose(0, 2, 1)
    Fn = jnp.cumsum(logf.astype(jnp.float32), axis=1).transpose(0, 2, 1)
    s_past = (jnp.einsum('bthd,bshd->bhts', q, k_past).astype(jnp.float32) * FOX_SCALE
              + Fn[..., :, None] + suffix[:, :, None, :])
    s_new = (jnp.einsum('bthd,bshd->bhts', q, k).astype(jnp.float32) * FOX_SCALE
             + Fn[..., :, None] - Fn[..., None, :])
    s_new = jnp.where(jnp.tril(jnp.ones((T, T), dtype=bool)), s_new, -jnp.inf)
    p_past, p_new = joint_softmax(s_past, s_new)
    dt = v.dtype
    return (jnp.einsum('bhts,bshd->bthd', p_past.astype(dt), v_past)
            + jnp.einsum('bhts,bshd->bthd', p_new.astype(dt), v))


def mlstm_chunkwise(q, k, v, ig, lf, c0, n0, m0):
    B, S, H, _ = q.shape
    L = MLSTM_CHUNK if S % MLSTM_CHUNK == 0 else S
    nc = S // L

    def to_chunks(a):
        a = a.astype(jnp.float32).reshape((B, nc, L) + a.shape[2:])
        return jnp.moveaxis(a, (1, 2), (0, 3))

    causal = jnp.tril(jnp.ones((L, L), dtype=bool))

    def step(carry, inp):
        C, n, m = carry
        qc, kc, vc, ic, fc = inp
        b = jnp.cumsum(fc, axis=-1)
        D = jnp.where(causal, b[..., :, None] - b[..., None, :] + ic[..., None, :], -jnp.inf)
        inter = b + m[..., None]
        mt = jnp.maximum(inter, jnp.max(D, axis=-1))
        w_inter = jnp.exp(inter - mt)
        A = jnp.exp(D - mt[..., None]) * jnp.einsum('bhtd,bhsd->bhts', qc, kc)
        num = w_inter[..., None] * jnp.einsum('bhtd,bhde->bhte', qc, C) + jnp.einsum('bhts,bhse->bhte', A, vc)
        den = w_inter * jnp.einsum('bhtd,bhd->bht', qc, n) + jnp.sum(A, axis=-1)
        h = num / jnp.maximum(jnp.abs(den), jnp.exp(-mt))[..., None]
        bl = b[..., -1]
        g = bl[..., None] - b + ic
        m_new = jnp.maximum(bl + m, jnp.max(g, axis=-1))
        wc = jnp.exp(bl + m - m_new)
        ws = jnp.exp(g - m_new[..., None])
        C_new = wc[..., None, None] * C + jnp.einsum('bhsd,bhse->bhde', kc * ws[..., None], vc)
        n_new = wc[..., None] * n + jnp.einsum('bhs,bhsd->bhd', ws, kc)
        return (C_new, n_new, m_new), h

    carry0 = (c0.astype(jnp.float32), n0.astype(jnp.float32), m0.astype(jnp.float32))
    (c1, n1, m1), h = lax.scan(step, carry0, (to_chunks(q), to_chunks(k), to_chunks(v), to_chunks(ig), to_chunks(lf)))
    h = jnp.moveaxis(h, (0, 3), (1, 2)).reshape(B, S, H, MLSTM_DV)
    return h, c1, n1, m1


def token_mixers(x, pos, lw, past):
    B, S, _ = x.shape
    offs = np.cumsum(IN_SPLITS)[:-1].tolist()
    (cq, ckv, kr, fq, fk, fv, ff, mq, mk, mv, mi, mf, mo) = jnp.split(x @ lw['w_in'], offs, axis=-1)
    cos, sin = rope_angles(pos)
    qf = (rms_norm(cq, lw['mla_q_norm']) @ lw['mla_w_uq']).reshape(B, S, MLA_HEADS, MLA_NOPE + MLA_ROPE)
    q_abs = jnp.einsum('bshn,chn->bhsc', qf[..., :MLA_NOPE], lw['mla_w_uk'])
    q_pe = apply_rope(qf[..., MLA_NOPE:], cos[:, None, :], sin[:, None, :]).transpose(0, 2, 1, 3)
    lat = rms_norm(ckv, lw['mla_kv_norm'])
    k_pe = apply_rope(kr, cos, sin)
    fq = fq.reshape(B, S, FOX_HEADS, FOX_DH)
    fk = fk.reshape(B, S, FOX_HEADS, FOX_DH)
    fv = fv.reshape(B, S, FOX_HEADS, FOX_DH)
    f_log = jax.nn.log_sigmoid(ff.astype(jnp.float32) + lw['fox_f_bias'])
    mq = mq.reshape(B, S, MLSTM_HEADS, MLSTM_DK)
    mk = mk.reshape(B, S, MLSTM_HEADS, MLSTM_DK) * MLSTM_DK ** -0.5
    mv = mv.reshape(B, S, MLSTM_HEADS, MLSTM_DV)
    m_ig = mi.astype(jnp.float32) + lw['mlstm_i_bias']
    m_lf = jax.nn.log_sigmoid(mf.astype(jnp.float32) + lw['mlstm_f_bias'])
    if past is None:
        o_lat = mla_prompt(q_abs, q_pe, lat, k_pe)
        o_fox = fox_prompt(fq, fk, fv, f_log)
        c0 = jnp.zeros((B, MLSTM_HEADS, MLSTM_DK, MLSTM_DV), jnp.float32)
        n0 = jnp.zeros((B, MLSTM_HEADS, MLSTM_DK), jnp.float32)
        m0 = jnp.zeros((B, MLSTM_HEADS), jnp.float32)
    else:
        lat_p, kpe_p, fk_p, fv_p, lf_p, c0, n0, m0 = past
        o_lat = mla_sample(q_abs, q_pe, lat, k_pe, lat_p, kpe_p)
        o_fox = fox_sample(fq, fk, fv, f_log, fk_p, fv_p, lf_p)
    h_ml, c1, n1, m1 = mlstm_chunkwise(mq, mk, mv, m_ig, m_lf, c0, n0, m0)
    o_mla = jnp.einsum('bhsc,chv->bshv', o_lat, lw['mla_w_uv']).reshape(B, S, MLA_WIDTH)
    o_ml = rms_norm(h_ml, lw['mlstm_norm']).astype(x.dtype).reshape(B, S, MLSTM_WIDTH) * jax.nn.sigmoid(mo)
    mixed = jnp.concatenate([o_mla, o_fox.reshape(B, S, FOX_WIDTH), o_ml], axis=-1) @ lw['w_out']
    return mixed, (lat, k_pe, fk, fv, f_log, c1, n1, m1)


def conv_ffn(x, conv_prev, w_up, conv_w, conv_b, w_down):
    S = x.shape[1]
    g, u = jnp.split(x @ w_up, 2, axis=-1)
    g_ext = jnp.concatenate([conv_prev.astype(g.dtype), g], axis=1)
    c = conv_b + conv_w[0] * g_ext[:, 0:S]
    for j in range(1, CONV_W):
        c = c + conv_w[j] * g_ext[:, j:j + S]
    h = jax.nn.gelu(c, approximate=False) * u
    return h @ w_down, g_ext[:, S:]


def trunk_layer(x, pe, pos, lw, past, conv_prev):
    mix, st = token_mixers(x, pos, lw, past)
    x = layer_norm(ALPHA * x + mix, lw['ln1_g'], lw['ln1_b'])
    f, conv_new = conv_ffn(x, conv_prev, lw['ffn_w_up'], lw['ffn_conv_w'], lw['ffn_conv_b'], lw['ffn_w_down'])
    x = layer_norm(ALPHA * x + f, lw['ln2_g'], lw['ln2_b'])
    x = x + jax.nn.sigmoid(x @ lw['ple_w_gate']) * (pe @ lw['ple_w_proj'])
    return x, st + (conv_new,)


def gather_pages(cache, page_table, layer):
    g = cache[page_table, layer]
    return g.reshape((g.shape[0], g.shape[1] * g.shape[2]) + g.shape[3:])


def stack_layers(per_layer):
    cols = list(zip(*per_layer))
    return [jnp.stack(c, axis=1 if j < 5 else 0) for j, c in enumerate(cols)]


def setup_inputs(seed: int = 0) -> dict:
    key = jax.random.key(seed)
    keys = jax.random.split(key, 48)
    counter = iter(range(48))

    def nrm(shape, scale=1.0):
        return jax.random.normal(keys[next(counter)], shape, jnp.float32) * scale

    n_pages = PAST_LEN // PAGE_SIZE
    n_used = DEC_BATCH * n_pages
    n_pool = n_used + n_used // 4
    page_table = jax.random.permutation(keys[next(counter)], n_pool)[:n_used].reshape(DEC_BATCH, n_pages).astype(jnp.int32)
    return {
        'x_prompt': nrm((BATCH, SEQ, D_MODEL)),
        'x_sample': nrm((DEC_BATCH, DEC_SEQ, D_MODEL)),
        'cache_mla_latent': nrm((n_pool, DEPTH, PAGE_SIZE, MLA_KV_LORA)),
        'cache_mla_rope': nrm((n_pool, DEPTH, PAGE_SIZE, MLA_ROPE)),
        'cache_fox_k': nrm((n_pool, DEPTH, PAGE_SIZE, FOX_HEADS, FOX_DH)),
        'cache_fox_v': nrm((n_pool, DEPTH, PAGE_SIZE, FOX_HEADS, FOX_DH)),
        'cache_fox_logf': jax.nn.log_sigmoid(nrm((n_pool, DEPTH, PAGE_SIZE, FOX_HEADS)) + FOX_F_BIAS_INIT),
        'state_mlstm_c': nrm((DEPTH, DEC_BATCH, MLSTM_HEADS, MLSTM_DK, MLSTM_DV), 0.5),
        'state_mlstm_n': nrm((DEPTH, DEC_BATCH, MLSTM_HEADS, MLSTM_DK), 0.5),
        'state_mlstm_m': nrm((DEPTH, DEC_BATCH, MLSTM_HEADS)),
        'state_ffn_conv': nrm((DEPTH, DEC_BATCH, CONV_W - 1, D_FF)),
        'page_table': page_table,
        'p_prompt': nrm((DEPTH, BATCH, SEQ, PLE_DIM)),
        'p_sample': nrm((DEPTH, DEC_BATCH, DEC_SEQ, PLE_DIM)),
        'w_in': nrm((DEPTH, D_MODEL, D_IN), D_MODEL ** -0.5),
        'mla_q_norm': 1.0 + nrm((DEPTH, MLA_Q_LORA), 0.02),
        'mla_w_uq': nrm((DEPTH, MLA_Q_LORA, MLA_HEADS * (MLA_NOPE + MLA_ROPE)), MLA_Q_LORA ** -0.5),
        'mla_kv_norm': 1.0 + nrm((DEPTH, MLA_KV_LORA), 0.02),
        'mla_w_uk': nrm((DEPTH, MLA_KV_LORA, MLA_HEADS, MLA_NOPE), MLA_KV_LORA ** -0.5),
        'mla_w_uv': nrm((DEPTH, MLA_KV_LORA, MLA_HEADS, MLA_V), MLA_KV_LORA ** -0.5),
        'fox_f_bias': FOX_F_BIAS_INIT + nrm((DEPTH, FOX_HEADS), 0.1),
        'mlstm_i_bias': nrm((DEPTH, MLSTM_HEADS), 0.1),
        'mlstm_f_bias': MLSTM_F_BIAS_INIT + nrm((DEPTH, MLSTM_HEADS), 0.1),
        'mlstm_norm': 1.0 + nrm((DEPTH, MLSTM_HEADS, MLSTM_DV), 0.02),
        'w_out': nrm((DEPTH, MIX_WIDTH, D_MODEL), MIX_WIDTH ** -0.5 * BETA),
        'ln1_g': 1.0 + nrm((DEPTH, D_MODEL), 0.02),
        'ln1_b': nrm((DEPTH, D_MODEL), 0.02),
        'ffn_w_up': nrm((DEPTH, D_MODEL, 2 * D_FF), D_MODEL ** -0.5),
        'ffn_conv_w': nrm((DEPTH, CONV_W, D_FF), CONV_W ** -0.5),
        'ffn_conv_b': nrm((DEPTH, D_FF), 0.02),
        'ffn_w_down': nrm((DEPTH, D_FF, D_MODEL), D_FF ** -0.5 * BETA),
        'ln2_g': 1.0 + nrm((DEPTH, D_MODEL), 0.02),
        'ln2_b': nrm((DEPTH, D_MODEL), 0.02),
        'ple_w_gate': nrm((DEPTH, D_MODEL, D_MODEL), D_MODEL ** -0.5),
        'ple_w_proj': nrm((DEPTH, PLE_DIM, D_MODEL), PLE_DIM ** -0.5),
    }


def reference(x_prompt, x_sample, cache_mla_latent, cache_mla_rope, cache_fox_k, cache_fox_v, cache_fox_logf,
              state_mlstm_c, state_mlstm_n, state_mlstm_m, state_ffn_conv, page_table, p_prompt, p_sample,
              w_in, mla_q_norm, mla_w_uq, mla_kv_norm, mla_w_uk, mla_w_uv, fox_f_bias, mlstm_i_bias, mlstm_f_bias,
              mlstm_norm, w_out, ln1_g, ln1_b, ffn_w_up, ffn_conv_w, ffn_conv_b, ffn_w_down, ln2_g, ln2_b,
              ple_w_gate, ple_w_proj):
    past_len = page_table.shape[1] * cache_mla_latent.shape[2]
    pos_p = jnp.arange(x_prompt.shape[1])
    pos_s = past_len + jnp.arange(x_sample.shape[1])
    conv_zero = jnp.zeros((x_prompt.shape[0], CONV_W - 1, D_FF), x_prompt.dtype)
    xp, xs = x_prompt, x_sample
    per_layer_p, per_layer_s = [], []
    for i in range(DEPTH):
        lw = {'w_in': w_in[i], 'mla_q_norm': mla_q_norm[i], 'mla_w_uq': mla_w_uq[i], 'mla_kv_norm': mla_kv_norm[i],
              'mla_w_uk': mla_w_uk[i], 'mla_w_uv': mla_w_uv[i], 'fox_f_bias': fox_f_bias[i],
              'mlstm_i_bias': mlstm_i_bias[i], 'mlstm_f_bias': mlstm_f_bias[i], 'mlstm_norm': mlstm_norm[i],
              'w_out': w_out[i], 'ln1_g': ln1_g[i], 'ln1_b': ln1_b[i], 'ffn_w_up': ffn_w_up[i],
              'ffn_conv_w': ffn_conv_w[i], 'ffn_conv_b': ffn_conv_b[i], 'ffn_w_down': ffn_w_down[i],
              'ln2_g': ln2_g[i], 'ln2_b': ln2_b[i], 'ple_w_gate': ple_w_gate[i], 'ple_w_proj': ple_w_proj[i]}
        xp, st_p = trunk_layer(xp, p_prompt[i], pos_p, lw, None, conv_zero)
        per_layer_p.append(st_p)
        past = (gather_pages(cache_mla_latent, page_table, i), gather_pages(cache_mla_rope, page_table, i),
                gather_pages(cache_fox_k, page_table, i), gather_pages(cache_fox_v, page_table, i),
                gather_pages(cache_fox_logf, page_table, i),
                state_mlstm_c[i], state_mlstm_n[i], state_mlstm_m[i])
        xs, st_s = trunk_layer(xs, p_sample[i], pos_s, lw, past, state_ffn_conv[i])
        per_layer_s.append(st_s)
    (lat_p, rope_p, fk_p, fv_p, lf_p, c_p, n_p, m_p, conv_p) = stack_layers(per_layer_p)
    (lat_s, rope_s, fk_s, fv_s, lf_s, c_s, n_s, m_s, conv_s) = stack_layers(per_layer_s)
    return (xp, xs, lat_p, rope_p, fk_p, fv_p, lf_p, c_p, n_p, m_p, conv_p,
            lat_s, rope_s, fk_s, fv_s, lf_s, c_s, n_s, m_s, conv_s)
```

```python
import functools
import math

import jax
import jax.numpy as jnp
from jax import lax
from jax.experimental import pallas as pl
from jax.experimental.pallas import tpu as pltpu

F32 = jnp.float32
BF16 = jnp.bfloat16
HI = lax.Precision.HIGHEST

MLA_HEADS = 6
MLA_NOPE = 64
MLA_ROPE = 32
MLA_V = 64
MLA_Q_LORA = 256
MLA_KV_LORA = 128
ROPE_THETA = 10000.0
MLA_SCALE = (MLA_NOPE + MLA_ROPE) ** -0.5
FOX_HEADS = 4
FOX_DH = 64
FOX_WIDTH = FOX_HEADS * FOX_DH
FOX_SCALE = FOX_DH ** -0.5
ML_HEADS = 6
ML_DK = 32
ML_DV = 64
ML_QW = ML_HEADS * ML_DK
ML_VW = ML_HEADS * ML_DV
ML_CHUNK = 128
CONV_W = 3
LN_EPS = 1e-5
RMS_EPS = 1e-6

LANE = 128
SUBLANE = 8
VMEM_LIMIT = 56 * 1024 * 1024

C_CQ = 0
C_CKV = 256
C_KR = 384
C_FQ = 512
C_FK = 768
C_FV = 1024
C_MQ = 1280
C_MK = 1536
C_G = 1792
C_MV = 1920
C_MO = 2304
D_EXT = 2688
G_FOX = 0
G_MI = 8
G_MF = 16
PAD_Q = 256


def _cparams(sem):
    return pltpu.CompilerParams(dimension_semantics=sem, vmem_limit_bytes=VMEM_LIMIT)


def _log_sigmoid(x):
    return jnp.minimum(x, 0.0) - jnp.log1p(jnp.exp(-jnp.abs(x)))


def _nt_dot(a, b, precision=None):
    return lax.dot_general(a, b, (((1,), (1,)), ((), ())), preferred_element_type=F32, precision=precision)


def _mm_kernel(x_ref, w_ref, o_ref):
    o_ref[...] = jnp.dot(x_ref[...].astype(BF16), w_ref[...], preferred_element_type=F32)


def in_projection(x, w_ext, tm):
    n, d = x.shape
    dout = w_ext.shape[1]
    return pl.pallas_call(
        _mm_kernel,
        grid=(n // tm,),
        in_specs=[pl.BlockSpec((tm, d), lambda i: (i, 0)),
                  pl.BlockSpec((d, dout), lambda i: (0, 0))],
        out_specs=pl.BlockSpec((tm, dout), lambda i: (i, 0)),
        out_shape=jax.ShapeDtypeStruct((n, dout), F32),
        compiler_params=_cparams(("arbitrary",)),
    )(x, w_ext)


def _mla_prep_kernel(p_ref, cos_ref, sin_ref, gq_ref, gkv_ref, wn_ref, wr_ref, wrs_ref, wuk_ref,
                     lat_ref, kpe_ref, kcat_ref, qcat_ref):
    cq = p_ref[:, C_CQ:C_CQ + MLA_Q_LORA]
    ckv = p_ref[:, C_CKV:C_CKV + MLA_KV_LORA]
    krb = p_ref[:, C_KR:C_KR + LANE]
    cos_t = cos_ref[...]
    sin_t = sin_ref[...]
    qn = (cq * lax.rsqrt(jnp.mean(cq * cq, axis=-1, keepdims=True) + RMS_EPS) * gq_ref[...]).astype(BF16)
    lat = ckv * lax.rsqrt(jnp.mean(ckv * ckv, axis=-1, keepdims=True) + RMS_EPS) * gkv_ref[...]
    kpe = krb * cos_t + pltpu.roll(krb, LANE - MLA_ROPE, axis=1) * sin_t
    lat_ref[...] = lat
    kpe_ref[...] = kpe
    kcat_ref[:, 0:LANE] = lat.astype(BF16)
    kcat_ref[:, LANE:2 * LANE] = kpe.astype(BF16)
    qnope = jnp.dot(qn, wn_ref[...], preferred_element_type=F32).astype(BF16)
    qabs = jnp.dot(qnope, wuk_ref[...], preferred_element_type=F32)
    qr = jnp.dot(qn, wr_ref[...], preferred_element_type=F32)
    qrs = jnp.dot(qn, wrs_ref[...], preferred_element_type=F32)
    for h in range(MLA_HEADS):
        sl = slice(h * LANE, (h + 1) * LANE)
        qpe = qr[:, sl] * cos_t + qrs[:, sl] * sin_t
        qcat_ref[h, :, 0:LANE] = (qabs[:, sl] * MLA_SCALE).astype(BF16)
        qcat_ref[h, :, LANE:2 * LANE] = (qpe * MLA_SCALE).astype(BF16)


def mla_prep(proj, cos_t, sin_t, gq, gkv, wn, wr, wrs, wuk, tm):
    n = proj.shape[0]
    t_tiles = cos_t.shape[0] // tm
    const = lambda i: (0, 0)
    return pl.pallas_call(
        _mla_prep_kernel,
        grid=(n // tm,),
        in_specs=[pl.BlockSpec((tm, 512), lambda i: (i, 0)),
                  pl.BlockSpec((tm, LANE), lambda i: (i % t_tiles, 0)),
                  pl.BlockSpec((tm, LANE), lambda i: (i % t_tiles, 0)),
                  pl.BlockSpec(gq.shape, const), pl.BlockSpec(gkv.shape, const),
                  pl.BlockSpec(wn.shape, const), pl.BlockSpec(wr.shape, const),
                  pl.BlockSpec(wrs.shape, const), pl.BlockSpec(wuk.shape, const)],
        out_specs=[pl.BlockSpec((tm, LANE), lambda i: (i, 0)),
                   pl.BlockSpec((tm, LANE), lambda i: (i, 0)),
                   pl.BlockSpec((tm, 2 * LANE), lambda i: (i, 0)),
                   pl.BlockSpec((MLA_HEADS, tm, 2 * LANE), lambda i: (0, i, 0))],
        out_shape=[jax.ShapeDtypeStruct((n, LANE), F32),
                   jax.ShapeDtypeStruct((n, LANE), F32),
                   jax.ShapeDtypeStruct((n, 2 * LANE), BF16),
                   jax.ShapeDtypeStruct((MLA_HEADS, n, 2 * LANE), BF16)],
        compiler_params=_cparams(("arbitrary",)),
    )(proj, cos_t, sin_t, gq, gkv, wn, wr, wrs, wuk)


def _fox_prep_kernel(k_ref, v_ref, g_ref, b_ref, kb_ref, vb_ref, fl_ref, fc_ref, carry, *, tiles_per_seq, seg):
    i = pl.program_id(0)
    tm = g_ref.shape[0]
    kb_ref[...] = k_ref[...].astype(BF16)
    vb_ref[...] = v_ref[...].astype(BF16)
    fl = _log_sigmoid(g_ref[...] + b_ref[...])
    fl_ref[...] = fl
    row = lax.broadcasted_iota(jnp.int32, (tm, tm), 0)
    col = lax.broadcasted_iota(jnp.int32, (tm, tm), 1)
    keep = col <= row
    if seg < tm:
        keep = jnp.logical_and(keep, (row // seg) == (col // seg))
    tri = jnp.where(keep, 1.0, 0.0).astype(F32)
    csum = jnp.dot(tri, fl, preferred_element_type=F32, precision=HI)

    @pl.when(i % tiles_per_seq == 0)
    def _():
        carry[...] = jnp.zeros_like(carry)

    csum = csum + carry[...]
    fc_ref[...] = csum
    carry[...] = csum[tm - 1:tm, :]


def fox_prep(proj, bias_row, seq_len, tm):
    n = proj.shape[0]
    tiles_per_seq = max(seq_len // tm, 1)
    kern = functools.partial(_fox_prep_kernel, tiles_per_seq=tiles_per_seq, seg=seq_len)
    return pl.pallas_call(
        kern,
        grid=(n // tm,),
        in_specs=[pl.BlockSpec((tm, FOX_WIDTH), lambda i: (i, C_FK // FOX_WIDTH)),
                  pl.BlockSpec((tm, FOX_WIDTH), lambda i: (i, C_FV // FOX_WIDTH)),
                  pl.BlockSpec((tm, LANE), lambda i: (i, C_G // LANE)),
                  pl.BlockSpec((1, LANE), lambda i: (0, 0))],
        out_specs=[pl.BlockSpec((tm, FOX_WIDTH), lambda i: (i, 0)),
                   pl.BlockSpec((tm, FOX_WIDTH), lambda i: (i, 0)),
                   pl.BlockSpec((tm, LANE), lambda i: (i, 0)),
                   pl.BlockSpec((tm, LANE), lambda i: (i, 0))],
        out_shape=[jax.ShapeDtypeStruct((n, FOX_WIDTH), BF16),
                   jax.ShapeDtypeStruct((n, FOX_WIDTH), BF16),
                   jax.ShapeDtypeStruct((n, LANE), F32),
                   jax.ShapeDtypeStruct((n, LANE), F32)],
        scratch_shapes=[pltpu.VMEM((1, LANE), F32)],
        compiler_params=_cparams(("arbitrary",)),
    )(proj, proj, proj, bias_row)


def _mla_flash_kernel(q_ref, k_ref, o_ref, m_s, l_s, acc_s, *, tq, tk):
    qi = pl.program_id(1)
    rows = MLA_HEADS * tq
    q = q_ref[...].reshape(rows, 2 * LANE)
    m_s[...] = jnp.full_like(m_s, -jnp.inf)
    l_s[...] = jnp.zeros_like(l_s)
    acc_s[...] = jnp.zeros_like(acc_s)

    def step(j, masked):
        kb = k_ref[pl.ds(pl.multiple_of(j * tk, tk), tk), :]
        s = _nt_dot(q, kb)
        if masked:
            qpos = qi * tq + lax.broadcasted_iota(jnp.int32, (MLA_HEADS, tq, tk), 1).reshape(rows, tk)
            kpos = j * tk + lax.broadcasted_iota(jnp.int32, (rows, tk), 1)
            s = jnp.where(kpos <= qpos, s, -jnp.inf)
        m_old = m_s[...]
        m_new = jnp.maximum(m_old, jnp.max(s, axis=-1, keepdims=True))
        a = jnp.exp(m_old - m_new)
        p = jnp.exp(s - m_new)
        l_s[...] = a * l_s[...] + jnp.sum(p, axis=-1, keepdims=True)
        acc_s[...] = a * acc_s[...] + jnp.dot(p.astype(BF16), kb[:, 0:LANE], preferred_element_type=F32)
        m_s[...] = m_new

    n_full = (qi * tq) // tk
    n_end = ((qi + 1) * tq + tk - 1) // tk

    def full_body(j, c):
        step(j, False)
        return c

    def diag_body(j, c):
        step(j, True)
        return c

    lax.fori_loop(0, n_full, full_body, 0)
    lax.fori_loop(n_full, n_end, diag_body, 0)
    o = acc_s[...] / l_s[...]
    for h in range(MLA_HEADS):
        o_ref[:, h * LANE:(h + 1) * LANE] = o[h * tq:(h + 1) * tq, :]


def mla_flash(qcat, kcat, batch, seq, tq, tk):
    n = kcat.shape[0]
    nq = seq // tq
    kern = functools.partial(_mla_flash_kernel, tq=tq, tk=tk)
    rows = MLA_HEADS * tq
    return pl.pallas_call(
        kern,
        grid=(batch, nq),
        in_specs=[pl.BlockSpec((MLA_HEADS, tq, 2 * LANE), lambda b, i: (0, b * nq + i, 0)),
                  pl.BlockSpec((seq, 2 * LANE), lambda b, i: (b, 0))],
        out_specs=pl.BlockSpec((tq, MLA_HEADS * LANE), lambda b, i: (b * nq + i, 0)),
        out_shape=jax.ShapeDtypeStruct((n, MLA_HEADS * LANE), F32),
        scratch_shapes=[pltpu.VMEM((rows, 1), F32), pltpu.VMEM((rows, 1), F32), pltpu.VMEM((rows, LANE), F32)],
        compiler_params=_cparams(("arbitrary", "arbitrary")),
    )(qcat, kcat)


def _fox_flash_kernel(q_ref, k_ref, v_ref, fc_ref, ft_ref, o_ref, m_s, l_s, acc_s, *, tq, tk):
    qi = pl.program_id(1)
    q = q_ref[...] * FOX_SCALE
    lane = lax.broadcasted_iota(jnp.int32, (1, FOX_WIDTH), 1)
    n_full = (qi * tq) // tk
    n_end = ((qi + 1) * tq + tk - 1) // tk
    out = jnp.zeros((tq, FOX_WIDTH), F32)
    for h in range(FOX_HEADS):
        hmask = (lane // FOX_DH) == h
        qh = jnp.where(hmask, q, 0.0).astype(BF16)
        f_t = fc_ref[:, G_FOX + h:G_FOX + h + 1]
        m_s[...] = jnp.full_like(m_s, -jnp.inf)
        l_s[...] = jnp.zeros_like(l_s)
        acc_s[...] = jnp.zeros_like(acc_s)

        def step(j, masked, qh=qh, f_t=f_t, h=h):
            start = pl.multiple_of(j * tk, tk)
            kb = k_ref[pl.ds(start, tk), :]
            vb = v_ref[pl.ds(start, tk), :]
            f_s = ft_ref[0, h:h + 1, pl.ds(start, tk)]
            s = _nt_dot(qh, kb) + (f_t - f_s)
            if masked:
                qpos = qi * tq + lax.broadcasted_iota(jnp.int32, (tq, tk), 0)
                kpos = j * tk + lax.broadcasted_iota(jnp.int32, (tq, tk), 1)
                s = jnp.where(kpos <= qpos, s, -jnp.inf)
            m_old = m_s[...]
            m_new = jnp.maximum(m_old, jnp.max(s, axis=-1, keepdims=True))
            a = jnp.exp(m_old - m_new)
            p = jnp.exp(s - m_new)
            l_s[...] = a * l_s[...] + jnp.sum(p, axis=-1, keepdims=True)
            acc_s[...] = a * acc_s[...] + jnp.dot(p.astype(BF16), vb, preferred_element_type=F32)
            m_s[...] = m_new

        def full_body(j, c, step=step):
            step(j, False)
            return c

        def diag_body(j, c, step=step):
            step(j, True)
            return c

        lax.fori_loop(0, n_full, full_body, 0)
        lax.fori_loop(n_full, n_end, diag_body, 0)
        out = jnp.where(hmask, acc_s[...] / l_s[...], out)
    o_ref[...] = out


def fox_flash(proj, kbf, vbf, fcum, fcum_t, batch, seq, tq, tk):
    n = proj.shape[0]
    nq = seq // tq
    kern = functools.partial(_fox_flash_kernel, tq=tq, tk=tk)
    return pl.pallas_call(
        kern,
        grid=(batch, nq),
        in_specs=[pl.BlockSpec((tq, FOX_WIDTH), lambda b, i: (b * nq + i, C_FQ // FOX_WIDTH)),
                  pl.BlockSpec((seq, FOX_WIDTH), lambda b, i: (b, 0)),
                  pl.BlockSpec((seq, FOX_WIDTH), lambda b, i: (b, 0)),
                  pl.BlockSpec((tq, LANE), lambda b, i: (b * nq + i, 0)),
                  pl.BlockSpec((1, SUBLANE, seq), lambda b, i: (b, 0, 0))],
        out_specs=pl.BlockSpec((tq, FOX_WIDTH), lambda b, i: (b * nq + i, 0)),
        out_shape=jax.ShapeDtypeStruct((n, FOX_WIDTH), F32),
        scratch_shapes=[pltpu.VMEM((tq, 1), F32), pltpu.VMEM((tq, 1), F32), pltpu.VMEM((tq, FOX_WIDTH), F32)],
        compiler_params=_cparams(("arbitrary", "arbitrary")),
    )(proj, kbf, vbf, fcum, fcum_t)


def _mlstm_kernel(q_ref, k_ref, v_ref, g_ref, mo_ref, ib_ref, fb_ref, gn_ref, c0_ref, n0_ref, m0_ref,
                  h_ref, c_ref, n_ref, m_ref, c_s, n_s, m_s, *, L):
    j = pl.program_id(1)

    @pl.when(j == 0)
    def _():
        c_s[...] = c0_ref[0]
        n_s[...] = n0_ref[0]
        m_s[...] = m0_ref[0]

    q = q_ref[...]
    k = k_ref[...] * (ML_DK ** -0.5)
    v = v_ref[...]
    g = g_ref[...]
    gi = g + ib_ref[...]
    lf = _log_sigmoid(g + fb_ref[...])
    row = lax.broadcasted_iota(jnp.int32, (L, L), 0)
    col = lax.broadcasted_iota(jnp.int32, (L, L), 1)
    causal = col <= row
    tri = jnp.where(causal, 1.0, 0.0).astype(F32)
    b = jnp.dot(tri, lf, preferred_element_type=F32, precision=HI)
    r8 = lax.broadcasted_iota(jnp.int32, (SUBLANE, LANE), 0)
    l8 = lax.broadcasted_iota(jnp.int32, (SUBLANE, LANE), 1)
    sel_f = jnp.where(l8 == r8 + G_MF, 1.0, 0.0).astype(F32)
    sel_i = jnp.where(l8 == r8 + G_MI, 1.0, 0.0).astype(F32)
    b_rows = _nt_dot(sel_f, b, precision=HI)
    i_rows = _nt_dot(sel_i, gi, precision=HI)

    c_old = c_s[...]
    n_old = n_s[...]
    m_old = m_s[...]
    qb = q.astype(BF16)
    kb = k.astype(BF16)
    vb = v.astype(BF16)
    qc_all = jnp.dot(qb, c_old.astype(BF16), preferred_element_type=F32)
    qn_all = q * n_old
    lane_q = lax.broadcasted_iota(jnp.int32, (1, PAD_Q), 1)
    lane_v = lax.broadcasted_iota(jnp.int32, (1, ML_VW), 1)
    lane_m = lax.broadcasted_iota(jnp.int32, (1, LANE), 1)
    b_last = b[L - 1:L, :]

    h_all = jnp.zeros((L, ML_VW), F32)
    kws = jnp.zeros((L, PAD_Q), F32)
    wc_v = jnp.zeros((1, ML_VW), F32)
    wc_q = jnp.zeros((1, PAD_Q), F32)
    m_new_row = m_old
    for h in range(ML_HEADS):
        mq = (lane_q // ML_DK) == h
        mv = (lane_v // ML_DV) == h
        bcol = b[:, G_MF + h:G_MF + h + 1]
        icol = gi[:, G_MI + h:G_MI + h + 1]
        brow = b_rows[h:h + 1, :]
        irow = i_rows[h:h + 1, :]
        m_h = m_old[:, h:h + 1]
        d = jnp.where(causal, bcol - brow + irow, -jnp.inf)
        inter = bcol + m_h
        mt = jnp.maximum(inter, jnp.max(d, axis=-1, keepdims=True))
        w_inter = jnp.exp(inter - mt)
        s = _nt_dot(jnp.where(mq, q, 0.0).astype(BF16), kb)
        a = jnp.exp(d - mt) * s
        av = jnp.dot(a.astype(BF16), vb, preferred_element_type=F32)
        qn = jnp.sum(jnp.where(mq, qn_all, 0.0), axis=-1, keepdims=True)
        den = w_inter * qn + jnp.sum(a, axis=-1, keepdims=True)
        hden = jnp.maximum(jnp.abs(den), jnp.exp(-mt))
        h_all = jnp.where(mv, (w_inter * qc_all + av) / hden, h_all)
        bl = b_last[:, G_MF + h:G_MF + h + 1]
        gg = bl - bcol + icol
        m_new = jnp.maximum(bl + m_h, jnp.max(gg, axis=0, keepdims=True))
        wc = jnp.exp(bl + m_h - m_new)
        ws = jnp.exp(gg - m_new)
        kws = jnp.where(mq, k * ws, kws)
        wc_v = jnp.where(mv, wc, wc_v)
        wc_q = jnp.where(mq, wc, wc_q)
        m_new_row = jnp.where(lane_m == h, m_new, m_new_row)

    kv = lax.dot_general(kws.astype(BF16), vb, (((0,), (0,)), ((), ())), preferred_element_type=F32)
    r_c = lax.broadcasted_iota(jnp.int32, (PAD_Q, ML_VW), 0)
    l_c = lax.broadcasted_iota(jnp.int32, (PAD_Q, ML_VW), 1)
    bd = (r_c // ML_DK) == (l_c // ML_DV)
    c_new = c_old * wc_v + jnp.where(bd, kv, 0.0)
    n_new = n_old * wc_q + jnp.sum(kws, axis=0, keepdims=True)
    c_s[...] = c_new
    n_s[...] = n_new
    m_s[...] = m_new_row
    c_ref[0] = c_new
    n_ref[0] = n_new
    m_ref[0] = m_new_row

    r_g = lax.broadcasted_iota(jnp.int32, (ML_VW, ML_VW), 0)
    l_g = lax.broadcasted_iota(jnp.int32, (ML_VW, ML_VW), 1)
    grp = jnp.where((r_g // ML_DV) == (l_g // ML_DV), 1.0 / ML_DV, 0.0).astype(F32)
    ms = jnp.dot(h_all * h_all, grp, preferred_element_type=F32, precision=HI)
    hn = h_all * lax.rsqrt(ms + RMS_EPS) * gn_ref[...]
    h_ref[...] = hn * jax.nn.sigmoid(mo_ref[...])


def mlstm(proj, ib_row, fb_row, gn_row, c0, n0, m0, batch, seq, L):
    n = proj.shape[0]
    nc = seq // L
    kern = functools.partial(_mlstm_kernel, L=L)
    const = lambda b, j: (0, 0)
    st = lambda b, j: (b, 0, 0)
    return pl.pallas_call(
        kern,
        grid=(batch, nc),
        in_specs=[pl.BlockSpec((L, PAD_Q), lambda b, j: (b * nc + j, C_MQ // PAD_Q)),
                  pl.BlockSpec((L, PAD_Q), lambda b, j: (b * nc + j, C_MK // PAD_Q)),
                  pl.BlockSpec((L, ML_VW), lambda b, j: (b * nc + j, C_MV // ML_VW)),
                  pl.BlockSpec((L, LANE), lambda b, j: (b * nc + j, C_G // LANE)),
                  pl.BlockSpec((L, ML_VW), lambda b, j: (b * nc + j, C_MO // ML_VW)),
                  pl.BlockSpec((1, LANE), const), pl.BlockSpec((1, LANE), const),
                  pl.BlockSpec((1, ML_VW), const),
                  pl.BlockSpec((1, PAD_Q, ML_VW), st), pl.BlockSpec((1, 1, PAD_Q), st),
                  pl.BlockSpec((1, 1, LANE), st)],
        out_specs=[pl.BlockSpec((L, ML_VW), lambda b, j: (b * nc + j, 0)),
                   pl.BlockSpec((1, PAD_Q, ML_VW), st), pl.BlockSpec((1, 1, PAD_Q), st),
                   pl.BlockSpec((1, 1, LANE), st)],
        out_shape=[jax.ShapeDtypeStruct((n, ML_VW), F32),
                   jax.ShapeDtypeStruct((batch, PAD_Q, ML_VW), F32),
                   jax.ShapeDtypeStruct((batch, 1, PAD_Q), F32),
                   jax.ShapeDtypeStruct((batch, 1, LANE), F32)],
        scratch_shapes=[pltpu.VMEM((PAD_Q, ML_VW), F32), pltpu.VMEM((1, PAD_Q), F32), pltpu.VMEM((1, LANE), F32)],
        compiler_params=_cparams(("arbitrary", "arbitrary")),
    )(proj, proj, proj, proj, proj, ib_row, fb_row, gn_row, c0, n0, m0)


def _layer_norm(y, g, b):
    mu = jnp.mean(y, axis=-1, keepdims=True)
    yc = y - mu
    var = jnp.mean(yc * yc, axis=-1, keepdims=True)
    return yc * lax.rsqrt(var + LN_EPS) * g + b


def _out_proj_kernel(ol_ref, of_ref, om_ref, x_ref, wuv_ref, wo_ref, g_ref, b_ref, o_ref, *, alpha):
    n_mla = MLA_HEADS * MLA_V
    t = jnp.dot(ol_ref[...].astype(BF16), wuv_ref[...], preferred_element_type=F32)
    mix = jnp.dot(t.astype(BF16), wo_ref[0:n_mla, :], preferred_element_type=F32)
    mix += jnp.dot(of_ref[...].astype(BF16), wo_ref[n_mla:n_mla + FOX_WIDTH, :], preferred_element_type=F32)
    mix += jnp.dot(om_ref[...].astype(BF16), wo_ref[n_mla + FOX_WIDTH:, :], preferred_element_type=F32)
    o_ref[...] = _layer_norm(alpha * x_ref[...] + mix, g_ref[...], b_ref[...])


def out_projection(olat, ofox, oml, x, wuv_bd, w_out, g, b, alpha, tm):
    n, d = x.shape
    const = lambda i: (0, 0)
    kern = functools.partial(_out_proj_kernel, alpha=alpha)
    return pl.pallas_call(
        kern,
        grid=(n // tm,),
        in_specs=[pl.BlockSpec((tm, olat.shape[1]), lambda i: (i, 0)),
                  pl.BlockSpec((tm, ofox.shape[1]), lambda i: (i, 0)),
                  pl.BlockSpec((tm, oml.shape[1]), lambda i: (i, 0)),
                  pl.BlockSpec((tm, d), lambda i: (i, 0)),
                  pl.BlockSpec(wuv_bd.shape, const), pl.BlockSpec(w_out.shape, const),
                  pl.BlockSpec((1, d), const), pl.BlockSpec((1, d), const)],
        out_specs=pl.BlockSpec((tm, d), lambda i: (i, 0)),
        out_shape=jax.ShapeDtypeStruct((n, d), F32),
        compiler_params=_cparams(("arbitrary",)),
    )(olat, ofox, oml, x, wuv_bd, w_out, g, b)


def _ffn_kernel(*refs, alpha, tm, fc, tiles_per_seq, short_seq):
    if short_seq:
        (x_ref, wg_ref, wu_ref, cw_ref, cb_ref, wd_ref, g_ref, b_ref, pe_ref, pg_ref, pp_ref, e1_ref, e2_ref,
         o_ref, gate_ref, acc_s, gext_s) = refs
    else:
        (x_ref, wg_ref, wu_ref, cw_ref, cb_ref, wd_ref, g_ref, b_ref, pe_ref, pg_ref, pp_ref, prev_ref,
         o_ref, cn_ref, acc_s, gext_s, carry_s) = refs
    i = pl.program_id(0)
    f = pl.program_id(1)
    nf = pl.num_programs(1)
    xb = x_ref[...].astype(BF16)
    gt = jnp.dot(xb, wg_ref[...], preferred_element_type=F32)
    up = jnp.dot(xb, wu_ref[...], preferred_element_type=F32)
    gext_s[SUBLANE:SUBLANE + tm, :] = gt
    if short_seq:
        gext_s[0:SUBLANE, :] = jnp.zeros((SUBLANE, fc), F32)
        gate_ref[...] = gt
        t_in = lax.broadcasted_iota(jnp.int32, (tm, 1), 0) % short_seq
        sh1 = jnp.where(t_in == 0, e1_ref[...], gext_s[SUBLANE - 1:SUBLANE - 1 + tm, :])
        sh2 = jnp.where(t_in < 2, e2_ref[...], gext_s[SUBLANE - 2:SUBLANE - 2 + tm, :])
    else:
        @pl.when(i % tiles_per_seq == 0)
        def _():
            gext_s[0:SUBLANE, :] = jnp.zeros((SUBLANE, fc), F32)
            gext_s[SUBLANE - 2:SUBLANE, :] = prev_ref[0]

        @pl.when(i % tiles_per_seq != 0)
        def _():
            gext_s[0:SUBLANE, :] = carry_s[f]

        carry_s[f] = gt[tm - SUBLANE:tm, :]
        cn_ref[0] = gt[tm - 2:tm, :]
        sh1 = gext_s[SUBLANE - 1:SUBLANE - 1 + tm, :]
        sh2 = gext_s[SUBLANE - 2:SUBLANE - 2 + tm, :]
    c = cb_ref[...] + cw_ref[0:1, :] * sh2 + cw_ref[1:2, :] * sh1 + cw_ref[2:3, :] * gt
    hh = 0.5 * c * (1.0 + lax.erf(c * (2.0 ** -0.5))) * up
    part = jnp.dot(hh.astype(BF16), wd_ref[...], preferred_element_type=F32)

    @pl.when(f == 0)
    def _():
        acc_s[...] = part

    @pl.when(f != 0)
    def _():
        acc_s[...] += part

    @pl.when(f == nf - 1)
    def _():
        y = _layer_norm(alpha * x_ref[...] + acc_s[...], g_ref[...], b_ref[...])
        gate = jax.nn.sigmoid(jnp.dot(y.astype(BF16), pg_ref[...], preferred_element_type=F32))
        o_ref[...] = y + gate * jnp.dot(pe_ref[...].astype(BF16), pp_ref[...], preferred_element_type=F32)


def conv_ffn(x, w_up, conv_w, conv_b, w_down, g, b, pe, ple_g, ple_p, alpha, tm, fc, *, seq_len, conv_prev=None,
             e1=None, e2=None):
    n, d = x.shape
    dff = w_down.shape[0]
    nf = dff // fc
    short_seq = seq_len if seq_len < tm else 0
    tiles_per_seq = max(seq_len // tm, 1)
    const = lambda i, f: (0, 0)
    in_specs = [pl.BlockSpec((tm, d), lambda i, f: (i, 0)),
                pl.BlockSpec((d, fc), lambda i, f: (0, f)),
                pl.BlockSpec((d, fc), lambda i, f: (0, nf + f)),
                pl.BlockSpec((CONV_W, fc), lambda i, f: (0, f)),
                pl.BlockSpec((1, fc), lambda i, f: (0, f)),
                pl.BlockSpec((fc, d), lambda i, f: (f, 0)),
                pl.BlockSpec((1, d), const), pl.BlockSpec((1, d), const),
                pl.BlockSpec((tm, pe.shape[1]), lambda i, f: (i, 0)),
                pl.BlockSpec(ple_g.shape, const), pl.BlockSpec(ple_p.shape, const)]
    args = [x, w_up, w_up, conv_w, conv_b, w_down, g, b, pe, ple_g, ple_p]
    scratch = [pltpu.VMEM((tm, d), F32), pltpu.VMEM((tm + SUBLANE, fc), F32)]
    if short_seq:
        in_specs += [pl.BlockSpec((tm, fc), lambda i, f: (i, f)), pl.BlockSpec((tm, fc), lambda i, f: (i, f))]
        args += [e1, e2]
        out_specs = [pl.BlockSpec((tm, d), lambda i, f: (i, 0)), pl.BlockSpec((tm, fc), lambda i, f: (i, f))]
        out_shape = [jax.ShapeDtypeStruct((n, d), F32), jax.ShapeDtypeStruct((n, dff), F32)]
    else:
        in_specs += [pl.BlockSpec((1, CONV_W - 1, fc), lambda i, f: (i // tiles_per_seq, 0, f))]
        args += [conv_prev]
        out_specs = [pl.BlockSpec((tm, d), lambda i, f: (i, 0)),
                     pl.BlockSpec((1, CONV_W - 1, fc), lambda i, f: (i, 0, f))]
        out_shape = [jax.ShapeDtypeStruct((n, d), F32),
                     jax.ShapeDtypeStruct((n // tm, CONV_W - 1, dff), F32)]
        scratch += [pltpu.VMEM((nf, SUBLANE, fc), F32)]
    kern = functools.partial(_ffn_kernel, alpha=alpha, tm=tm, fc=fc, tiles_per_seq=tiles_per_seq,
                             short_seq=short_seq)
    return pl.pallas_call(
        kern, grid=(n // tm, nf), in_specs=in_specs, out_specs=out_specs, out_shape=out_shape,
        scratch_shapes=scratch, compiler_params=_cparams(("arbitrary", "arbitrary")),
    )(*args)


def _sample_attn_kernel(pt_ref,
                        qc_ref, fq_ref, kcn_ref, fkn_ref, fvn_ref, fln_ref,
                        lat_hbm, rope_hbm, fk_hbm, fv_hbm, lf_hbm,
                        ol_ref, of_ref,
                        lat_b, rope_b, fk_b, fv_b, lf_b, sems,
                        ml_m, ml_l, ml_acc, fx_m, fx_l, fx_acc, fx_p,
                        *, layer, pb, nblk, nseq, t_new):
    b = pl.program_id(0)
    j = pl.program_id(1)
    step = b * nblk + j
    slot = step % 2
    page = lat_b.shape[2]

    def copies(bb, jj, sl):
        out = []
        for p in range(pb):
            pg = pt_ref[bb, jj * pb + p]
            cols = pl.ds(p * page, page)
            out.append(pltpu.make_async_copy(lat_hbm.at[pg, layer], lat_b.at[sl, p], sems.at[sl, 0]))
            out.append(pltpu.make_async_copy(rope_hbm.at[pg, layer], rope_b.at[sl, :, cols], sems.at[sl, 1]))
            out.append(pltpu.make_async_copy(fk_hbm.at[pg, layer], fk_b.at[sl, :, cols], sems.at[sl, 2]))
            out.append(pltpu.make_async_copy(fv_hbm.at[pg, layer], fv_b.at[sl, :, cols], sems.at[sl, 3]))
            out.append(pltpu.make_async_copy(lf_hbm.at[pg, layer], lf_b.at[sl, :, cols], sems.at[sl, 4]))
        return out

    @pl.when(step == 0)
    def _():
        for c in copies(b, j, slot):
            c.start()

    nxt = step + 1
    nb = jnp.where(j == nblk - 1, b + 1, b)
    nj = jnp.where(j == nblk - 1, 0, j + 1)

    @pl.when(nxt < nseq * nblk)
    def _():
        for c in copies(nb, nj, 1 - slot):
            c.start()

    rows_m = MLA_HEADS * t_new
    rows_f = FOX_HEADS * t_new
    keys = pb * page

    @pl.when(j == 0)
    def _():
        ml_m[...] = jnp.full_like(ml_m, -jnp.inf)
        ml_l[...] = jnp.zeros_like(ml_l)
        ml_acc[...] = jnp.zeros_like(ml_acc)
        fx_m[...] = jnp.full_like(fx_m, -jnp.inf)
        fx_l[...] = jnp.zeros_like(fx_l)
        fx_acc[...] = jnp.zeros_like(fx_acc)
        fx_p[...] = jnp.zeros_like(fx_p)

    for c in copies(b, j, slot):
        c.wait()

    qc = qc_ref[...].reshape(rows_m, 2 * LANE)
    lane_f = lax.broadcasted_iota(jnp.int32, (1, FOX_WIDTH), 1)
    fq = fq_ref[...] * FOX_SCALE
    qf = jnp.concatenate([jnp.where((lane_f // FOX_DH) == h, fq, 0.0) for h in range(FOX_HEADS)],
                         axis=0).astype(BF16)
    r8 = lax.broadcasted_iota(jnp.int32, (SUBLANE, FOX_HEADS), 0)
    l8 = lax.broadcasted_iota(jnp.int32, (SUBLANE, FOX_HEADS), 1)
    sel = jnp.where(r8 == l8, 1.0, 0.0).astype(F32)

    def lane_cumsum(x):
        n = x.shape[1]
        idx = lax.broadcasted_iota(jnp.int32, x.shape, 1)
        sh = 1
        while sh < n:
            x = x + jnp.where(idx >= sh, pltpu.roll(x, sh, axis=1), 0.0)
            sh *= 2
        return x

    def expand_rows(x8):
        return jnp.concatenate([jnp.broadcast_to(x8[h:h + 1, :], (t_new, x8.shape[1])) for h in range(FOX_HEADS)],
                               axis=0)

    def online(m_s, l_s, acc_s, s, v, v_feature_major=False):
        m_old = m_s[...]
        m_new = jnp.maximum(m_old, jnp.max(s, axis=-1, keepdims=True))
        a = jnp.exp(m_old - m_new)
        p = jnp.exp(s - m_new)
        l_s[...] = a * l_s[...] + jnp.sum(p, axis=-1, keepdims=True)
        p = p.astype(BF16)
        pv = _nt_dot(p, v) if v_feature_major else jnp.dot(p, v, preferred_element_type=F32)
        acc_s[...] = a * acc_s[...] + pv
        m_s[...] = m_new

    latb = lat_b[slot].reshape(keys, MLA_KV_LORA).astype(BF16)
    rope_t = rope_b[slot].astype(BF16)
    s_m = _nt_dot(qc[:, 0:LANE], latb) + jnp.dot(qc[:, LANE:LANE + MLA_ROPE], rope_t, preferred_element_type=F32)
    online(ml_m, ml_l, ml_acc, s_m, latb)

    fk_t = fk_b[slot].astype(BF16)
    fv_t = fv_b[slot].astype(BF16)
    pre = lane_cumsum(lf_b[slot]) + fx_p[...]
    fx_p[...] = pre[:, keys - 1:keys]
    s_f = jnp.dot(qf, fk_t, preferred_element_type=F32) - expand_rows(pre)
    online(fx_m, fx_l, fx_acc, s_f, fv_t, v_feature_major=True)

    @pl.when(j == nblk - 1)
    def _():
        kcn = kcn_ref[...]
        tq_m = lax.broadcasted_iota(jnp.int32, (MLA_HEADS, t_new, t_new), 1).reshape(rows_m, t_new)
        ts_m = lax.broadcasted_iota(jnp.int32, (rows_m, t_new), 1)
        s_n = jnp.where(ts_m <= tq_m, _nt_dot(qc, kcn), -jnp.inf)
        online(ml_m, ml_l, ml_acc, s_n, kcn[:, 0:LANE])
        o_m = ml_acc[...] / ml_l[...]
        for h in range(MLA_HEADS):
            ol_ref[:, h * LANE:(h + 1) * LANE] = o_m[h * t_new:(h + 1) * t_new, :]

        total = expand_rows(fx_p[...])
        fx_m[...] = fx_m[...] + total
        fcum = lane_cumsum(_nt_dot(sel, fln_ref[:, G_FOX:G_FOX + FOX_HEADS], precision=HI))
        tq_f = lax.broadcasted_iota(jnp.int32, (FOX_HEADS, t_new, t_new), 1).reshape(rows_f, t_new)
        ts_f = lax.broadcasted_iota(jnp.int32, (rows_f, t_new), 1)
        s_n = jnp.where(ts_f <= tq_f, _nt_dot(qf, fkn_ref[...]) - expand_rows(fcum), -jnp.inf)
        online(fx_m, fx_l, fx_acc, s_n, fvn_ref[...])
        o_f = fx_acc[...] / fx_l[...]
        out = jnp.zeros((t_new, FOX_WIDTH), F32)
        for h in range(FOX_HEADS):
            out = jnp.where((lane_f // FOX_DH) == h, o_f[h * t_new:(h + 1) * t_new, :], out)
        of_ref[...] = out


def sample_attention(page_table, qcat, proj, kcat, fkb, fvb, flog, caches, layer, nseq, t_new, pb):
    lat_c, rope_c, fk_c, fv_c, lf_c = caches
    n_pages = page_table.shape[1]
    page = lat_c.shape[2]
    nblk = n_pages // pb
    keys = pb * page
    rope_c = jnp.transpose(rope_c, (0, 1, 3, 2))
    fk_c = jnp.transpose(fk_c, (0, 1, 3, 4, 2)).reshape(fk_c.shape[0], fk_c.shape[1], FOX_WIDTH, page)
    fv_c = jnp.transpose(fv_c, (0, 1, 3, 4, 2)).reshape(fv_c.shape[0], fv_c.shape[1], FOX_WIDTH, page)
    lf_c = jnp.pad(jnp.transpose(lf_c, (0, 1, 3, 2)), ((0, 0), (0, 0), (0, SUBLANE - FOX_HEADS), (0, 0)))
    rows_m = MLA_HEADS * t_new
    rows_f = FOX_HEADS * t_new
    kern = functools.partial(_sample_attn_kernel, layer=layer, pb=pb, nblk=nblk, nseq=nseq, t_new=t_new)
    any_spec = pl.BlockSpec(memory_space=pl.ANY)
    grid_spec = pltpu.PrefetchScalarGridSpec(
        num_scalar_prefetch=1,
        grid=(nseq, nblk),
        in_specs=[pl.BlockSpec((MLA_HEADS, t_new, 2 * LANE), lambda b, j, pt: (0, b, 0)),
                  pl.BlockSpec((t_new, FOX_WIDTH), lambda b, j, pt: (b, C_FQ // FOX_WIDTH)),
                  pl.BlockSpec((t_new, 2 * LANE), lambda b, j, pt: (b, 0)),
                  pl.BlockSpec((t_new, FOX_WIDTH), lambda b, j, pt: (b, 0)),
                  pl.BlockSpec((t_new, FOX_WIDTH), lambda b, j, pt: (b, 0)),
                  pl.BlockSpec((t_new, LANE), lambda b, j, pt: (b, 0)),
                  any_spec, any_spec, any_spec, any_spec, any_spec],
        out_specs=[pl.BlockSpec((t_new, MLA_HEADS * LANE), lambda b, j, pt: (b, 0)),
                   pl.BlockSpec((t_new, FOX_WIDTH), lambda b, j, pt: (b, 0))],
        scratch_shapes=[pltpu.VMEM((2, pb, page, MLA_KV_LORA), F32),
                        pltpu.VMEM((2, MLA_ROPE, keys), F32),
                        pltpu.VMEM((2, FOX_WIDTH, keys), F32),
                        pltpu.VMEM((2, FOX_WIDTH, keys), F32),
                        pltpu.VMEM((2, SUBLANE, keys), F32),
                        pltpu.SemaphoreType.DMA((2, 5)),
                        pltpu.VMEM((rows_m, 1), F32), pltpu.VMEM((rows_m, 1), F32),
                        pltpu.VMEM((rows_m, LANE), F32),
                        pltpu.VMEM((rows_f, 1), F32), pltpu.VMEM((rows_f, 1), F32),
                        pltpu.VMEM((rows_f, FOX_WIDTH), F32),
                        pltpu.VMEM((SUBLANE, 1), F32)],
    )
    n = nseq * t_new
    return pl.pallas_call(
        kern,
        grid_spec=grid_spec,
        out_shape=[jax.ShapeDtypeStruct((n, MLA_HEADS * LANE), F32),
                   jax.ShapeDtypeStruct((n, FOX_WIDTH), F32)],
        compiler_params=_cparams(("arbitrary", "arbitrary")),
    )(page_table, qcat, proj, kcat, fkb, fvb, flog, lat_c, rope_c, fk_c, fv_c, lf_c)


def _pad_cols(w, width):
    return jnp.pad(w, ((0, 0), (0, width - w.shape[1])))


def _prep_layer_weights(w_in, mla_w_uq, mla_w_uk, mla_w_uv):
    d = w_in.shape[0]
    splits = (MLA_Q_LORA, MLA_KV_LORA, MLA_ROPE, FOX_WIDTH, FOX_WIDTH, FOX_WIDTH, FOX_HEADS,
              ML_QW, ML_QW, ML_VW, ML_HEADS, ML_HEADS, ML_VW)
    offs = [0]
    for s in splits:
        offs.append(offs[-1] + s)
    (cq, ckv, kr, fq, fk, fv, ff, mq, mk, mv, mi, mf, mo) = [w_in[:, offs[i]:offs[i + 1]] for i in range(len(splits))]
    half = MLA_ROPE // 2
    swap = lambda w: jnp.concatenate([w[..., half:], w[..., :half]], axis=-1)
    kr_blk = _pad_cols(jnp.concatenate([kr, swap(kr)], axis=1), LANE)
    gate_blk = jnp.concatenate([_pad_cols(ff, G_MI - G_FOX), _pad_cols(mi, G_MF - G_MI), _pad_cols(mf, LANE - G_MF)],
                               axis=1)
    w_ext = jnp.concatenate([cq, ckv, kr_blk, fq, fk, fv, _pad_cols(mq, PAD_Q), _pad_cols(mk, PAD_Q), gate_blk,
                             mv, mo], axis=1).astype(BF16)
    assert w_ext.shape == (d, D_EXT)
    uq = mla_w_uq.reshape(MLA_Q_LORA, MLA_HEADS, MLA_NOPE + MLA_ROPE)
    wn = uq[:, :, :MLA_NOPE].reshape(MLA_Q_LORA, MLA_HEADS * MLA_NOPE)
    rope = uq[:, :, MLA_NOPE:]
    pad_r = lambda w: jnp.pad(w, ((0, 0), (0, 0), (0, LANE - MLA_ROPE))).reshape(MLA_Q_LORA, MLA_HEADS * LANE)
    wr = pad_r(rope)
    wrs = pad_r(swap(rope))
    eye = jnp.eye(MLA_HEADS, dtype=F32)
    uk = jnp.transpose(mla_w_uk, (1, 2, 0))
    wuk_bd = (uk[:, :, None, :] * eye[:, None, :, None]).reshape(MLA_HEADS * MLA_NOPE, MLA_HEADS * MLA_KV_LORA)
    uv = jnp.transpose(mla_w_uv, (1, 0, 2))
    wuv_bd = (uv[:, :, None, :] * eye[:, None, :, None]).reshape(MLA_HEADS * MLA_KV_LORA, MLA_HEADS * MLA_V)
    return w_ext, wn.astype(BF16), wr.astype(BF16), wrs.astype(BF16), wuk_bd.astype(BF16), wuv_bd.astype(BF16)


def _rope_tables(pos, reps):
    inv_freq = 1.0 / (ROPE_THETA ** (jnp.arange(0, MLA_ROPE, 2, dtype=F32) / MLA_ROPE))
    ang = pos.astype(F32)[:, None] * inv_freq[None, :]
    cos, sin = jnp.cos(ang), jnp.sin(ang)
    cos_t = _pad_cols(jnp.concatenate([cos, cos], axis=1), LANE)
    sin_t = _pad_cols(jnp.concatenate([-sin, sin], axis=1), LANE)
    return jnp.tile(cos_t, (reps, 1)), jnp.tile(sin_t, (reps, 1))


def _gate_row(vec, off):
    return jnp.zeros((1, LANE), F32).at[0, off:off + vec.shape[0]].set(vec)


def _state_to_blockdiag(c, n, m):
    bsz = c.shape[0]
    eye = jnp.eye(ML_HEADS, dtype=F32)
    cbd = (c[:, :, :, None, :] * eye[None, :, None, :, None]).reshape(bsz, ML_QW, ML_VW)
    cbd = jnp.pad(cbd, ((0, 0), (0, PAD_Q - ML_QW), (0, 0)))
    nrow = jnp.pad(n.reshape(bsz, 1, ML_QW), ((0, 0), (0, 0), (0, PAD_Q - ML_QW)))
    mrow = jnp.pad(m.reshape(bsz, 1, ML_HEADS), ((0, 0), (0, 0), (0, LANE - ML_HEADS)))
    return cbd, nrow, mrow


def _blockdiag_to_state(cbd, nrow, mrow):
    bsz = cbd.shape[0]
    c5 = cbd[:, :ML_QW, :].reshape(bsz, ML_HEADS, ML_DK, ML_HEADS, ML_DV)
    c = jnp.stack([c5[:, h, :, h, :] for h in range(ML_HEADS)], axis=1)
    n = nrow[:, 0, :ML_QW].reshape(bsz, ML_HEADS, ML_DK)
    m = mrow[:, 0, :ML_HEADS]
    return c, n, m


def _run_group(x, pe, lw, *, batch, seq, pos, alpha, tm, past):
    n = x.shape[0]
    w_ext, wn, wr, wrs, wuk_bd, wuv_bd = lw['prepped']
    proj = in_projection(x, w_ext, tm)
    reps = 1 if past is None else n // seq
    cos_t, sin_t = _rope_tables(pos, reps)
    lat, kpe, kcat, qcat = mla_prep(proj, cos_t, sin_t, lw['mla_q_norm'][None, :], lw['mla_kv_norm'][None, :],
                                    wn, wr, wrs, wuk_bd, tm)
    fox_bias = _gate_row(lw['fox_f_bias'], G_FOX)
    kbf, vbf, flog, fcum = fox_prep(proj, fox_bias, seq, tm)
    ib_row = _gate_row(lw['mlstm_i_bias'], G_MI)
    fb_row = _gate_row(lw['mlstm_f_bias'], G_MF)
    gn_row = lw['mlstm_norm'].reshape(1, ML_VW)
    if past is None:
        olat = mla_flash(qcat, kcat, batch, seq, tq=256, tk=512)
        fcum_t = jnp.transpose(fcum.reshape(batch, seq, LANE)[:, :, :SUBLANE], (0, 2, 1))
        ofox = fox_flash(proj, kbf, vbf, fcum, fcum_t, batch, seq, tq=256, tk=512)
        c0 = jnp.zeros((batch, PAD_Q, ML_VW), F32)
        n0 = jnp.zeros((batch, 1, PAD_Q), F32)
        m0 = jnp.zeros((batch, 1, LANE), F32)
        chunk = ML_CHUNK if seq % ML_CHUNK == 0 else seq
    else:
        olat, ofox = sample_attention(past['page_table'], qcat, proj, kcat, kbf, vbf, flog, past['caches'],
                                      past['layer'], batch, seq, pb=16)
        c0, n0, m0 = _state_to_blockdiag(past['c'], past['n'], past['m'])
        chunk = ML_CHUNK if seq % ML_CHUNK == 0 else seq
    oml, c1, n1, m1 = mlstm(proj, ib_row, fb_row, gn_row, c0, n0, m0, batch, seq, chunk)
    c1, n1, m1 = _blockdiag_to_state(c1, n1, m1)
    x1 = out_projection(olat, ofox, oml, x, wuv_bd, lw['w_out'], lw['ln1_g'][None, :], lw['ln1_b'][None, :], alpha, tm)
    dff = lw['ffn_w_down'].shape[0]
    ffn_args = (x1, lw['ffn_w_up'], lw['ffn_conv_w'], lw['ffn_conv_b'][None, :], lw['ffn_w_down'],
                lw['ln2_g'][None, :], lw['ln2_b'][None, :], pe, lw['ple_w_gate'], lw['ple_w_proj'], alpha, tm, 256)
    if past is None:
        x2, tails = conv_ffn(*ffn_args, seq_len=seq, conv_prev=jnp.zeros((batch, CONV_W - 1, dff), F32))
        conv_new = tails.reshape(batch, seq // tm, CONV_W - 1, dff)[:, -1]
    else:
        prev = past['conv']
        e1 = jnp.pad(prev[:, 1:2], ((0, 0), (0, seq - 1), (0, 0))).reshape(n, dff)
        e2 = jnp.pad(prev, ((0, 0), (0, seq - 2), (0, 0))).reshape(n, dff)
        x2, gate = conv_ffn(*ffn_args, seq_len=seq, e1=e1, e2=e2)
        conv_new = gate.reshape(batch, seq, dff)[:, seq - (CONV_W - 1):]
    d_lat = lat.reshape(batch, seq, MLA_KV_LORA)
    d_rope = kpe[:, :MLA_ROPE].reshape(batch, seq, MLA_ROPE)
    d_fk = proj[:, C_FK:C_FK + FOX_WIDTH].reshape(batch, seq, FOX_HEADS, FOX_DH)
    d_fv = proj[:, C_FV:C_FV + FOX_WIDTH].reshape(batch, seq, FOX_HEADS, FOX_DH)
    d_lf = flog[:, G_FOX:G_FOX + FOX_HEADS].reshape(batch, seq, FOX_HEADS)
    return x2, (d_lat, d_rope, d_fk, d_fv, d_lf, c1, n1, m1, conv_new)


def kernel(x_prompt, x_sample, cache_mla_latent, cache_mla_rope, cache_fox_k, cache_fox_v, cache_fox_logf,
           state_mlstm_c, state_mlstm_n, state_mlstm_m, state_ffn_conv, page_table, p_prompt, p_sample,
           w_in, mla_q_norm, mla_w_uq, mla_kv_norm, mla_w_uk, mla_w_uv, fox_f_bias, mlstm_i_bias, mlstm_f_bias,
           mlstm_norm, w_out, ln1_g, ln1_b, ffn_w_up, ffn_conv_w, ffn_conv_b, ffn_w_down, ln2_g, ln2_b,
           ple_w_gate, ple_w_proj):
    depth = w_in.shape[0]
    bp, sp, d = x_prompt.shape
    bs, ss, _ = x_sample.shape
    past_len = page_table.shape[1] * cache_mla_latent.shape[2]
    alpha = (2 * depth) ** 0.25
    pos_p = jnp.arange(sp)
    pos_s = past_len + jnp.arange(ss)
    xp = x_prompt.reshape(bp * sp, d)
    xs = x_sample.reshape(bs * ss, d)
    caches = (cache_mla_latent, cache_mla_rope, cache_fox_k, cache_fox_v, cache_fox_logf)
    per_p, per_s = [], []
    for i in range(depth):
        lw = {'prepped': _prep_layer_weights(w_in[i], mla_w_uq[i], mla_w_uk[i], mla_w_uv[i]),
              'mla_q_norm': mla_q_norm[i], 'mla_kv_norm': mla_kv_norm[i], 'fox_f_bias': fox_f_bias[i],
              'mlstm_i_bias': mlstm_i_bias[i], 'mlstm_f_bias': mlstm_f_bias[i], 'mlstm_norm': mlstm_norm[i],
              'w_out': w_out[i].astype(BF16), 'ln1_g': ln1_g[i], 'ln1_b': ln1_b[i],
              'ffn_w_up': ffn_w_up[i].astype(BF16), 'ffn_conv_w': ffn_conv_w[i], 'ffn_conv_b': ffn_conv_b[i],
              'ffn_w_down': ffn_w_down[i].astype(BF16), 'ln2_g': ln2_g[i], 'ln2_b': ln2_b[i],
              'ple_w_gate': ple_w_gate[i].astype(BF16), 'ple_w_proj': ple_w_proj[i].astype(BF16)}
        xp, st_p = _run_group(xp, p_prompt[i].reshape(bp * sp, -1), lw, batch=bp, seq=sp, pos=pos_p, alpha=alpha,
                              tm=min(512, bp * sp), past=None)
        per_p.append(st_p)
        past = {'page_table': page_table, 'caches': caches, 'layer': i, 'c': state_mlstm_c[i],
                'n': state_mlstm_n[i], 'm': state_mlstm_m[i], 'conv': state_ffn_conv[i]}
        xs, st_s = _run_group(xs, p_sample[i].reshape(bs * ss, -1), lw, batch=bs, seq=ss, pos=pos_s, alpha=alpha,
                              tm=min(512, bs * ss), past=past)
        per_s.append(st_s)

    def stack(per_layer):
        cols = list(zip(*per_layer))
        return [jnp.stack(c, axis=1 if k < 5 else 0) for k, c in enumerate(cols)]

    return tuple([xp.reshape(bp, sp, d), xs.reshape(bs, ss, d)] + stack(per_p) + stack(per_s))
```

```python
import functools
import math

import jax
import jax.numpy as jnp
from jax import lax
from jax.experimental import pallas as pl
from jax.experimental.pallas import tpu as pltpu

F32 = jnp.float32
BF16 = jnp.bfloat16
HI = lax.Precision.HIGHEST

MLA_HEADS = 6
MLA_NOPE = 64
MLA_ROPE = 32
MLA_V = 64
MLA_Q_LORA = 256
MLA_KV_LORA = 128
ROPE_THETA = 10000.0
MLA_SCALE = (MLA_NOPE + MLA_ROPE) ** -0.5
FOX_HEADS = 4
FOX_DH = 64
FOX_WIDTH = FOX_HEADS * FOX_DH
FOX_SCALE = FOX_DH ** -0.5
LOG2E = math.log2(math.e)
MLA_QSCALE = MLA_SCALE * LOG2E
FOX_QSCALE = FOX_SCALE * LOG2E
ML_HEADS = 6
ML_DK = 32
ML_DV = 64
ML_QW = ML_HEADS * ML_DK
ML_VW = ML_HEADS * ML_DV
ML_CHUNK = 128
CONV_W = 3
LN_EPS = 1e-5
RMS_EPS = 1e-6

LANE = 128
SUBLANE = 8
VMEM_LIMIT = 56 * 1024 * 1024

C_CQ = 0
C_CKV = 256
C_KR = 384
C_FQ = 512
C_FK = 768
C_FV = 1024
C_MQ = 1280
C_MK = 1536
C_G = 1792
C_MV = 1920
C_MO = 2304
D_EXT = 2688
G_FOX = 0
G_MI = 8
G_MF = 16
PAD_Q = 256


def _cparams(sem):
    return pltpu.CompilerParams(dimension_semantics=sem, vmem_limit_bytes=VMEM_LIMIT)


def _log_sigmoid(x):
    return jnp.minimum(x, 0.0) - jnp.log1p(jnp.exp(-jnp.abs(x)))


def _nt_dot(a, b, precision=None):
    return lax.dot_general(a, b, (((1,), (1,)), ((), ())), preferred_element_type=F32, precision=precision)


def _mm_kernel(x_ref, w_ref, o_ref):
    o_ref[...] = jnp.dot(x_ref[...].astype(BF16), w_ref[...], preferred_element_type=F32)


def in_projection(x, w_ext, tm):
    n, d = x.shape
    dout = w_ext.shape[1]
    return pl.pallas_call(
        _mm_kernel,
        grid=(n // tm,),
        in_specs=[pl.BlockSpec((tm, d), lambda i: (i, 0)),
                  pl.BlockSpec((d, dout), lambda i: (0, 0))],
        out_specs=pl.BlockSpec((tm, dout), lambda i: (i, 0)),
        out_shape=jax.ShapeDtypeStruct((n, dout), F32),
        compiler_params=_cparams(("arbitrary",)),
    )(x, w_ext)


def _mla_prep_kernel(p_ref, cos_ref, sin_ref, gq_ref, gkv_ref, wn_ref, wr_ref, wrs_ref, wuk_ref,
                     lat_ref, kpe_ref, kcat_ref, qcat_ref):
    cq = p_ref[:, C_CQ:C_CQ + MLA_Q_LORA]
    ckv = p_ref[:, C_CKV:C_CKV + MLA_KV_LORA]
    krb = p_ref[:, C_KR:C_KR + LANE]
    cos_t = cos_ref[...]
    sin_t = sin_ref[...]
    qn = (cq * lax.rsqrt(jnp.mean(cq * cq, axis=-1, keepdims=True) + RMS_EPS) * gq_ref[...]).astype(BF16)
    lat = ckv * lax.rsqrt(jnp.mean(ckv * ckv, axis=-1, keepdims=True) + RMS_EPS) * gkv_ref[...]
    kpe = krb * cos_t + pltpu.roll(krb, LANE - MLA_ROPE, axis=1) * sin_t
    lat_ref[...] = lat
    kpe_ref[...] = kpe
    kcat_ref[:, 0:LANE] = lat.astype(BF16)
    kcat_ref[:, LANE:2 * LANE] = kpe.astype(BF16)
    qnope = jnp.dot(qn, wn_ref[...], preferred_element_type=F32).astype(BF16)
    qabs = jnp.dot(qnope, wuk_ref[...], preferred_element_type=F32)
    qr = jnp.dot(qn, wr_ref[...], preferred_element_type=F32)
    qrs = jnp.dot(qn, wrs_ref[...], preferred_element_type=F32)
    for h in range(MLA_HEADS):
        sl = slice(h * LANE, (h + 1) * LANE)
        qpe = qr[:, sl] * cos_t + qrs[:, sl] * sin_t
        qcat_ref[h, :, 0:LANE] = (qabs[:, sl] * MLA_QSCALE).astype(BF16)
        qcat_ref[h, :, LANE:2 * LANE] = (qpe * MLA_QSCALE).astype(BF16)


def mla_prep(proj, cos_t, sin_t, gq, gkv, wn, wr, wrs, wuk, tm):
    n = proj.shape[0]
    t_tiles = cos_t.shape[0] // tm
    const = lambda i: (0, 0)
    return pl.pallas_call(
        _mla_prep_kernel,
        grid=(n // tm,),
        in_specs=[pl.BlockSpec((tm, 512), lambda i: (i, 0)),
                  pl.BlockSpec((tm, LANE), lambda i: (i % t_tiles, 0)),
                  pl.BlockSpec((tm, LANE), lambda i: (i % t_tiles, 0)),
                  pl.BlockSpec(gq.shape, const), pl.BlockSpec(gkv.shape, const),
                  pl.BlockSpec(wn.shape, const), pl.BlockSpec(wr.shape, const),
                  pl.BlockSpec(wrs.shape, const), pl.BlockSpec(wuk.shape, const)],
        out_specs=[pl.BlockSpec((tm, LANE), lambda i: (i, 0)),
                   pl.BlockSpec((tm, LANE), lambda i: (i, 0)),
                   pl.BlockSpec((tm, 2 * LANE), lambda i: (i, 0)),
                   pl.BlockSpec((MLA_HEADS, tm, 2 * LANE), lambda i: (0, i, 0))],
        out_shape=[jax.ShapeDtypeStruct((n, LANE), F32),
                   jax.ShapeDtypeStruct((n, LANE), F32),
                   jax.ShapeDtypeStruct((n, 2 * LANE), BF16),
                   jax.ShapeDtypeStruct((MLA_HEADS, n, 2 * LANE), BF16)],
        compiler_params=_cparams(("arbitrary",)),
    )(proj, cos_t, sin_t, gq, gkv, wn, wr, wrs, wuk)


def _fox_prep_kernel(k_ref, v_ref, g_ref, b_ref, kb_ref, vb_ref, fl_ref, fc_ref, carry, *, tiles_per_seq, seg):
    i = pl.program_id(0)
    tm = g_ref.shape[0]
    kb_ref[...] = k_ref[...].astype(BF16)
    vb_ref[...] = v_ref[...].astype(BF16)
    fl = _log_sigmoid(g_ref[...] + b_ref[...])
    fl_ref[...] = fl
    row = lax.broadcasted_iota(jnp.int32, (tm, tm), 0)
    col = lax.broadcasted_iota(jnp.int32, (tm, tm), 1)
    keep = col <= row
    if seg < tm:
        keep = jnp.logical_and(keep, (row // seg) == (col // seg))
    tri = jnp.where(keep, 1.0, 0.0).astype(F32)
    csum = jnp.dot(tri, fl, preferred_element_type=F32, precision=HI)

    @pl.when(i % tiles_per_seq == 0)
    def _():
        carry[...] = jnp.zeros_like(carry)

    csum = csum + carry[...]
    fc_ref[...] = csum
    carry[...] = csum[tm - 1:tm, :]


def fox_prep(proj, bias_row, seq_len, tm):
    n = proj.shape[0]
    tiles_per_seq = max(seq_len // tm, 1)
    kern = functools.partial(_fox_prep_kernel, tiles_per_seq=tiles_per_seq, seg=seq_len)
    return pl.pallas_call(
        kern,
        grid=(n // tm,),
        in_specs=[pl.BlockSpec((tm, FOX_WIDTH), lambda i: (i, C_FK // FOX_WIDTH)),
                  pl.BlockSpec((tm, FOX_WIDTH), lambda i: (i, C_FV // FOX_WIDTH)),
                  pl.BlockSpec((tm, LANE), lambda i: (i, C_G // LANE)),
                  pl.BlockSpec((1, LANE), lambda i: (0, 0))],
        out_specs=[pl.BlockSpec((tm, FOX_WIDTH), lambda i: (i, 0)),
                   pl.BlockSpec((tm, FOX_WIDTH), lambda i: (i, 0)),
                   pl.BlockSpec((tm, LANE), lambda i: (i, 0)),
                   pl.BlockSpec((tm, LANE), lambda i: (i, 0))],
        out_shape=[jax.ShapeDtypeStruct((n, FOX_WIDTH), BF16),
                   jax.ShapeDtypeStruct((n, FOX_WIDTH), BF16),
                   jax.ShapeDtypeStruct((n, LANE), F32),
                   jax.ShapeDtypeStruct((n, LANE), F32)],
        scratch_shapes=[pltpu.VMEM((1, LANE), F32)],
        compiler_params=_cparams(("arbitrary",)),
    )(proj, proj, proj, bias_row)


def _softmax_step(s, v, m_s, l_s, acc_s):
    tk = s.shape[1]
    m_old = m_s[...]
    m_new = jnp.maximum(m_old, jnp.max(s, axis=-1, keepdims=True))
    a = jnp.exp2(m_old - m_new)
    lsum = a * l_s[...]
    ps = []
    for c in range(tk // LANE):
        p_c = jnp.exp2(s[:, c * LANE:(c + 1) * LANE] - m_new)
        lsum = lsum + p_c
        ps.append(p_c.astype(BF16))
    p = jnp.concatenate(ps, axis=1) if len(ps) > 1 else ps[0]
    pv = jnp.dot(p, v, preferred_element_type=F32)
    dv = pv.shape[1]
    a_full = a if dv == LANE else jnp.concatenate([a] * (dv // LANE), axis=1)
    acc_s[...] = a_full * acc_s[...] + pv
    l_s[...] = lsum
    m_s[...] = m_new


def _causal_loops(qi, tq, tk, step):
    n_full = (qi * tq) // tk
    n_end = ((qi + 1) * tq + tk - 1) // tk

    def full_body(j, c):
        step(j, False)
        return c

    def diag_body(j, c):
        step(j, True)
        return c

    lax.fori_loop(0, n_full, full_body, 0)
    lax.fori_loop(n_full, n_end, diag_body, 0)


def _mla_flash_kernel(q_ref, k_ref, o_ref, m_s, l_s, acc_s, *, tq, tk):
    qi = pl.program_id(1)
    rows = MLA_HEADS * tq
    q = q_ref[...].reshape(rows, 2 * LANE)
    m_s[...] = jnp.full_like(m_s, -jnp.inf)
    l_s[...] = jnp.zeros_like(l_s)
    acc_s[...] = jnp.zeros_like(acc_s)

    def step(j, masked):
        kb = k_ref[pl.ds(pl.multiple_of(j * tk, tk), tk), :]
        s = _nt_dot(q, kb)
        if masked:
            qpos = qi * tq + lax.broadcasted_iota(jnp.int32, (MLA_HEADS, tq, tk), 1).reshape(rows, tk)
            kpos = j * tk + lax.broadcasted_iota(jnp.int32, (rows, tk), 1)
            s = jnp.where(kpos <= qpos, s, -jnp.inf)
        _softmax_step(s, kb[:, 0:LANE], m_s, l_s, acc_s)

    _causal_loops(qi, tq, tk, step)
    o = acc_s[...] / jnp.sum(l_s[...], axis=-1, keepdims=True)
    for h in range(MLA_HEADS):
        o_ref[:, h * LANE:(h + 1) * LANE] = o[h * tq:(h + 1) * tq, :]


def mla_flash(qcat, kcat, batch, seq, tq, tk):
    n = kcat.shape[0]
    nq = seq // tq
    kern = functools.partial(_mla_flash_kernel, tq=tq, tk=tk)
    rows = MLA_HEADS * tq
    return pl.pallas_call(
        kern,
        grid=(batch, nq),
        in_specs=[pl.BlockSpec((MLA_HEADS, tq, 2 * LANE), lambda b, i: (0, b * nq + i, 0)),
                  pl.BlockSpec((seq, 2 * LANE), lambda b, i: (b, 0))],
        out_specs=pl.BlockSpec((tq, MLA_HEADS * LANE), lambda b, i: (b * nq + i, 0)),
        out_shape=jax.ShapeDtypeStruct((n, MLA_HEADS * LANE), F32),
        scratch_shapes=[pltpu.VMEM((rows, LANE), F32), pltpu.VMEM((rows, LANE), F32), pltpu.VMEM((rows, LANE), F32)],
        compiler_params=_cparams(("arbitrary", "arbitrary")),
    )(qcat, kcat)


def _fox_flash_kernel(q_ref, k_ref, v_ref, ft_ref, o_ref, m_s, l_s, acc_s, *, tq, tk):
    qi = pl.program_id(1)
    rows = FOX_HEADS * tq
    q = q_ref[...] * FOX_QSCALE
    lane = lax.broadcasted_iota(jnp.int32, (1, FOX_WIDTH), 1)
    qs = jnp.concatenate([jnp.where((lane // FOX_DH) == h, q, 0.0) for h in range(FOX_HEADS)],
                         axis=0).astype(BF16)
    m_s[...] = jnp.full_like(m_s, -jnp.inf)
    l_s[...] = jnp.zeros_like(l_s)
    acc_s[...] = jnp.zeros_like(acc_s)

    def step(j, masked):
        start = pl.multiple_of(j * tk, tk)
        kb = k_ref[pl.ds(start, tk), :]
        vb = v_ref[pl.ds(start, tk), :]
        qk = _nt_dot(qs, kb)
        s = jnp.concatenate([qk[h * tq:(h + 1) * tq, :] - ft_ref[0, h:h + 1, pl.ds(start, tk)] * LOG2E
                             for h in range(FOX_HEADS)], axis=0)
        if masked:
            qpos = qi * tq + lax.broadcasted_iota(jnp.int32, (FOX_HEADS, tq, tk), 1).reshape(rows, tk)
            kpos = j * tk + lax.broadcasted_iota(jnp.int32, (rows, tk), 1)
            s = jnp.where(kpos <= qpos, s, -jnp.inf)
        _softmax_step(s, vb, m_s, l_s, acc_s)

    _causal_loops(qi, tq, tk, step)
    o = acc_s[...] / jnp.sum(l_s[...], axis=-1, keepdims=True)
    out = o[0:tq, :]
    for h in range(1, FOX_HEADS):
        out = jnp.where((lane // FOX_DH) == h, o[h * tq:(h + 1) * tq, :], out)
    o_ref[...] = out


def fox_flash(proj, kbf, vbf, fcum_t, batch, seq, tq, tk):
    n = proj.shape[0]
    nq = seq // tq
    rows = FOX_HEADS * tq
    kern = functools.partial(_fox_flash_kernel, tq=tq, tk=tk)
    return pl.pallas_call(
        kern,
        grid=(batch, nq),
        in_specs=[pl.BlockSpec((tq, FOX_WIDTH), lambda b, i: (b * nq + i, C_FQ // FOX_WIDTH)),
                  pl.BlockSpec((seq, FOX_WIDTH), lambda b, i: (b, 0)),
                  pl.BlockSpec((seq, FOX_WIDTH), lambda b, i: (b, 0)),
                  pl.BlockSpec((1, SUBLANE, seq), lambda b, i: (b, 0, 0))],
        out_specs=pl.BlockSpec((tq, FOX_WIDTH), lambda b, i: (b * nq + i, 0)),
        out_shape=jax.ShapeDtypeStruct((n, FOX_WIDTH), F32),
        scratch_shapes=[pltpu.VMEM((rows, LANE), F32), pltpu.VMEM((rows, LANE), F32),
                        pltpu.VMEM((rows, FOX_WIDTH), F32)],
        compiler_params=_cparams(("arbitrary", "arbitrary")),
    )(proj, kbf, vbf, fcum_t)


def _mlstm_chunk(q, k, v, g, mo, ib, fb, gn, c_old, n_old, m_old, L):
    k = k * (ML_DK ** -0.5)
    gi = g + ib
    lf = _log_sigmoid(g + fb)
    row = lax.broadcasted_iota(jnp.int32, (L, L), 0)
    col = lax.broadcasted_iota(jnp.int32, (L, L), 1)
    causal = col <= row
    tri = jnp.where(causal, 1.0, 0.0).astype(F32)
    b = jnp.dot(tri, lf, preferred_element_type=F32, precision=HI)
    r8 = lax.broadcasted_iota(jnp.int32, (SUBLANE, LANE), 0)
    l8 = lax.broadcasted_iota(jnp.int32, (SUBLANE, LANE), 1)
    sel_f = jnp.where(l8 == r8 + G_MF, 1.0, 0.0).astype(F32)
    sel_i = jnp.where(l8 == r8 + G_MI, 1.0, 0.0).astype(F32)
    b_rows = _nt_dot(sel_f, b, precision=HI)
    i_rows = _nt_dot(sel_i, gi, precision=HI)

    qb = q.astype(BF16)
    kb = k.astype(BF16)
    vb = v.astype(BF16)
    qc_all = jnp.dot(qb, c_old.astype(BF16), preferred_element_type=F32)
    qn_all = q * n_old
    lane_q = lax.broadcasted_iota(jnp.int32, (1, PAD_Q), 1)
    lane_v = lax.broadcasted_iota(jnp.int32, (1, ML_VW), 1)
    lane_m = lax.broadcasted_iota(jnp.int32, (1, LANE), 1)
    b_last = b[L - 1:L, :]

    h_all = jnp.zeros((L, ML_VW), F32)
    kws = jnp.zeros((L, PAD_Q), F32)
    wc_v = jnp.zeros((1, ML_VW), F32)
    wc_q = jnp.zeros((1, PAD_Q), F32)
    m_new_row = m_old
    for h in range(ML_HEADS):
        mq = (lane_q // ML_DK) == h
        mv = (lane_v // ML_DV) == h
        bcol = b[:, G_MF + h:G_MF + h + 1]
        icol = gi[:, G_MI + h:G_MI + h + 1]
        brow = b_rows[h:h + 1, :]
        irow = i_rows[h:h + 1, :]
        m_h = m_old[:, h:h + 1]
        d = jnp.where(causal, bcol - brow + irow, -jnp.inf)
        inter = bcol + m_h
        mt = jnp.maximum(inter, jnp.max(d, axis=-1, keepdims=True))
        w_inter = jnp.exp(inter - mt)
        s = _nt_dot(jnp.where(mq, q, 0.0).astype(BF16), kb)
        a = jnp.exp(d - mt) * s
        av = jnp.dot(a.astype(BF16), vb, preferred_element_type=F32)
        qn = jnp.sum(jnp.where(mq, qn_all, 0.0), axis=-1, keepdims=True)
        den = w_inter * qn + jnp.sum(a, axis=-1, keepdims=True)
        hden = jnp.maximum(jnp.abs(den), jnp.exp(-mt))
        h_all = jnp.where(mv, (w_inter * qc_all + av) / hden, h_all)
        bl = b_last[:, G_MF + h:G_MF + h + 1]
        gg = bl - bcol + icol
        m_new = jnp.maximum(bl + m_h, jnp.max(gg, axis=0, keepdims=True))
        wc = jnp.exp(bl + m_h - m_new)
        ws = jnp.exp(gg - m_new)
        kws = jnp.where(mq, k * ws, kws)
        wc_v = jnp.where(mv, wc, wc_v)
        wc_q = jnp.where(mq, wc, wc_q)
        m_new_row = jnp.where(lane_m == h, m_new, m_new_row)

    kv = lax.dot_general(kws.astype(BF16), vb, (((0,), (0,)), ((), ())), preferred_element_type=F32)
    r_c = lax.broadcasted_iota(jnp.int32, (PAD_Q, ML_VW), 0)
    l_c = lax.broadcasted_iota(jnp.int32, (PAD_Q, ML_VW), 1)
    bd = (r_c // ML_DK) == (l_c // ML_DV)
    c_new = c_old * wc_v + jnp.where(bd, kv, 0.0)
    n_new = n_old * wc_q + jnp.sum(kws, axis=0, keepdims=True)

    r_g = lax.broadcasted_iota(jnp.int32, (ML_VW, ML_VW), 0)
    l_g = lax.broadcasted_iota(jnp.int32, (ML_VW, ML_VW), 1)
    grp = jnp.where((r_g // ML_DV) == (l_g // ML_DV), 1.0 / ML_DV, 0.0).astype(BF16)
    h2 = h_all * h_all
    h2_hi = h2.astype(BF16)
    h2_lo = (h2 - h2_hi.astype(F32)).astype(BF16)
    ms = jnp.dot(h2_hi, grp, preferred_element_type=F32) + jnp.dot(h2_lo, grp, preferred_element_type=F32)
    hn = h_all * lax.rsqrt(ms + RMS_EPS) * gn
    return hn * jax.nn.sigmoid(mo), c_new, n_new, m_new_row


def _mlstm_kernel(q_ref, k_ref, v_ref, g_ref, mo_ref, ib_ref, fb_ref, gn_ref, c0_ref, n0_ref, m0_ref,
                  h_ref, c_ref, n_ref, m_ref, c_s, n_s, m_s, *, L, nb):
    j = pl.program_id(1)

    @pl.when(j == 0)
    def _():
        c_s[...] = c0_ref[...]
        n_s[...] = n0_ref[...]
        m_s[...] = m0_ref[...]

    for bi in range(nb):
        h, c_new, n_new, m_new = _mlstm_chunk(q_ref[bi], k_ref[bi], v_ref[bi], g_ref[bi], mo_ref[bi], ib_ref[...],
                                              fb_ref[...], gn_ref[...], c_s[bi], n_s[bi], m_s[bi], L)
        h_ref[bi] = h
        c_s[bi] = c_new
        n_s[bi] = n_new
        m_s[bi] = m_new
        c_ref[bi] = c_new
        n_ref[bi] = n_new
        m_ref[bi] = m_new


def mlstm(proj, ib_row, fb_row, gn_row, c0, n0, m0, batch, seq, L, nb):
    n, dext = proj.shape
    nc = seq // L
    proj3 = proj.reshape(batch, seq, dext)
    kern = functools.partial(_mlstm_kernel, L=L, nb=nb)
    const = lambda b, j: (0, 0)
    st = lambda b, j: (b, 0, 0)
    col = lambda c: (lambda b, j: (b, j, c))
    h, c1, n1, m1 = pl.pallas_call(
        kern,
        grid=(batch // nb, nc),
        in_specs=[pl.BlockSpec((nb, L, PAD_Q), col(C_MQ // PAD_Q)),
                  pl.BlockSpec((nb, L, PAD_Q), col(C_MK // PAD_Q)),
                  pl.BlockSpec((nb, L, ML_VW), col(C_MV // ML_VW)),
                  pl.BlockSpec((nb, L, LANE), col(C_G // LANE)),
                  pl.BlockSpec((nb, L, ML_VW), col(C_MO // ML_VW)),
                  pl.BlockSpec((1, LANE), const), pl.BlockSpec((1, LANE), const),
                  pl.BlockSpec((1, ML_VW), const),
                  pl.BlockSpec((nb, PAD_Q, ML_VW), st), pl.BlockSpec((nb, 1, PAD_Q), st),
                  pl.BlockSpec((nb, 1, LANE), st)],
        out_specs=[pl.BlockSpec((nb, L, ML_VW), lambda b, j: (b, j, 0)),
                   pl.BlockSpec((nb, PAD_Q, ML_VW), st), pl.BlockSpec((nb, 1, PAD_Q), st),
                   pl.BlockSpec((nb, 1, LANE), st)],
        out_shape=[jax.ShapeDtypeStruct((batch, seq, ML_VW), F32),
                   jax.ShapeDtypeStruct((batch, PAD_Q, ML_VW), F32),
                   jax.ShapeDtypeStruct((batch, 1, PAD_Q), F32),
                   jax.ShapeDtypeStruct((batch, 1, LANE), F32)],
        scratch_shapes=[pltpu.VMEM((nb, PAD_Q, ML_VW), F32), pltpu.VMEM((nb, 1, PAD_Q), F32),
                        pltpu.VMEM((nb, 1, LANE), F32)],
        compiler_params=_cparams(("arbitrary", "arbitrary")),
    )(proj3, proj3, proj3, proj3, proj3, ib_row, fb_row, gn_row, c0, n0, m0)
    return h.reshape(n, ML_VW), c1, n1, m1


def _layer_norm(y, g, b):
    mu = jnp.mean(y, axis=-1, keepdims=True)
    yc = y - mu
    var = jnp.mean(yc * yc, axis=-1, keepdims=True)
    return yc * lax.rsqrt(var + LN_EPS) * g + b


def _out_proj_kernel(ol_ref, of_ref, om_ref, x_ref, wuv_ref, wo_ref, g_ref, b_ref, o_ref, *, alpha):
    n_mla = MLA_HEADS * MLA_V
    t = jnp.dot(ol_ref[...].astype(BF16), wuv_ref[...], preferred_element_type=F32)
    mix = jnp.dot(t.astype(BF16), wo_ref[0:n_mla, :], preferred_element_type=F32)
    mix += jnp.dot(of_ref[...].astype(BF16), wo_ref[n_mla:n_mla + FOX_WIDTH, :], preferred_element_type=F32)
    mix += jnp.dot(om_ref[...].astype(BF16), wo_ref[n_mla + FOX_WIDTH:, :], preferred_element_type=F32)
    o_ref[...] = _layer_norm(alpha * x_ref[...] + mix, g_ref[...], b_ref[...])


def out_projection(olat, ofox, oml, x, wuv_bd, w_out, g, b, alpha, tm):
    n, d = x.shape
    const = lambda i: (0, 0)
    kern = functools.partial(_out_proj_kernel, alpha=alpha)
    return pl.pallas_call(
        kern,
        grid=(n // tm,),
        in_specs=[pl.BlockSpec((tm, olat.shape[1]), lambda i: (i, 0)),
                  pl.BlockSpec((tm, ofox.shape[1]), lambda i: (i, 0)),
                  pl.BlockSpec((tm, oml.shape[1]), lambda i: (i, 0)),
                  pl.BlockSpec((tm, d), lambda i: (i, 0)),
                  pl.BlockSpec(wuv_bd.shape, const), pl.BlockSpec(w_out.shape, const),
                  pl.BlockSpec((1, d), const), pl.BlockSpec((1, d), const)],
        out_specs=pl.BlockSpec((tm, d), lambda i: (i, 0)),
        out_shape=jax.ShapeDtypeStruct((n, d), F32),
        compiler_params=_cparams(("arbitrary",)),
    )(olat, ofox, oml, x, wuv_bd, w_out, g, b)


def _ffn_kernel(*refs, alpha, tm, fc, tiles_per_seq, short_seq):
    if short_seq:
        (x_ref, wg_ref, wu_ref, cw_ref, cb_ref, wd_ref, g_ref, b_ref, pe_ref, pg_ref, pp_ref, e1_ref, e2_ref,
         o_ref, gate_ref, acc_s, gext_s) = refs
    else:
        (x_ref, wg_ref, wu_ref, cw_ref, cb_ref, wd_ref, g_ref, b_ref, pe_ref, pg_ref, pp_ref, prev_ref,
         o_ref, cn_ref, acc_s, gext_s, carry_s) = refs
    i = pl.program_id(0)
    f = pl.program_id(1)
    nf = pl.num_programs(1)
    xb = x_ref[...].astype(BF16)
    gt = jnp.dot(xb, wg_ref[...], preferred_element_type=F32)
    up = jnp.dot(xb, wu_ref[...], preferred_element_type=F32)
    gext_s[SUBLANE:SUBLANE + tm, :] = gt
    if short_seq:
        gext_s[0:SUBLANE, :] = jnp.zeros((SUBLANE, fc), F32)
        gate_ref[...] = gt
        t_in = lax.broadcasted_iota(jnp.int32, (tm, 1), 0) % short_seq
        sh1 = jnp.where(t_in == 0, e1_ref[...], gext_s[SUBLANE - 1:SUBLANE - 1 + tm, :])
        sh2 = jnp.where(t_in < 2, e2_ref[...], gext_s[SUBLANE - 2:SUBLANE - 2 + tm, :])
    else:
        @pl.when(i % tiles_per_seq == 0)
        def _():
            gext_s[0:SUBLANE, :] = jnp.zeros((SUBLANE, fc), F32)
            gext_s[SUBLANE - 2:SUBLANE, :] = prev_ref[0]

        @pl.when(i % tiles_per_seq != 0)
        def _():
            gext_s[0:SUBLANE, :] = carry_s[f]

        carry_s[f] = gt[tm - SUBLANE:tm, :]
        cn_ref[0] = gt[tm - 2:tm, :]
        sh1 = gext_s[SUBLANE - 1:SUBLANE - 1 + tm, :]
        sh2 = gext_s[SUBLANE - 2:SUBLANE - 2 + tm, :]
    c = cb_ref[...] + cw_ref[0:1, :] * sh2 + cw_ref[1:2, :] * sh1 + cw_ref[2:3, :] * gt
    hh = 0.5 * c * (1.0 + lax.erf(c * (2.0 ** -0.5))) * up
    part = jnp.dot(hh.astype(BF16), wd_ref[...], preferred_element_type=F32)

    @pl.when(f == 0)
    def _():
        acc_s[...] = part

    @pl.when(f != 0)
    def _():
        acc_s[...] += part

    @pl.when(f == nf - 1)
    def _():
        y = _layer_norm(alpha * x_ref[...] + acc_s[...], g_ref[...], b_ref[...])
        gate = jax.nn.sigmoid(jnp.dot(y.astype(BF16), pg_ref[...], preferred_element_type=F32))
        o_ref[...] = y + gate * jnp.dot(pe_ref[...].astype(BF16), pp_ref[...], preferred_element_type=F32)


def _ffn_chunk(dff, wide):
    half = dff // 2
    if wide and dff % 2 == 0 and half % LANE == 0:
        return half
    return 256 if dff % 256 == 0 else dff


def conv_ffn(x, w_up, conv_w, conv_b, w_down, g, b, pe, ple_g, ple_p, alpha, tm, fc, *, seq_len, conv_prev=None,
             e1=None, e2=None):
    n, d = x.shape
    dff = w_down.shape[0]
    nf = dff // fc
    short_seq = seq_len if seq_len < tm else 0
    tiles_per_seq = max(seq_len // tm, 1)
    const = lambda i, f: (0, 0)
    in_specs = [pl.BlockSpec((tm, d), lambda i, f: (i, 0)),
                pl.BlockSpec((d, fc), lambda i, f: (0, f)),
                pl.BlockSpec((d, fc), lambda i, f: (0, nf + f)),
                pl.BlockSpec((CONV_W, fc), lambda i, f: (0, f)),
                pl.BlockSpec((1, fc), lambda i, f: (0, f)),
                pl.BlockSpec((fc, d), lambda i, f: (f, 0)),
                pl.BlockSpec((1, d), const), pl.BlockSpec((1, d), const),
                pl.BlockSpec((tm, pe.shape[1]), lambda i, f: (i, 0)),
                pl.BlockSpec(ple_g.shape, const), pl.BlockSpec(ple_p.shape, const)]
    args = [x, w_up, w_up, conv_w, conv_b, w_down, g, b, pe, ple_g, ple_p]
    scratch = [pltpu.VMEM((tm, d), F32), pltpu.VMEM((tm + SUBLANE, fc), F32)]
    if short_seq:
        in_specs += [pl.BlockSpec((tm, fc), lambda i, f: (i, f)), pl.BlockSpec((tm, fc), lambda i, f: (i, f))]
        args += [e1, e2]
        out_specs = [pl.BlockSpec((tm, d), lambda i, f: (i, 0)), pl.BlockSpec((tm, fc), lambda i, f: (i, f))]
        out_shape = [jax.ShapeDtypeStruct((n, d), F32), jax.ShapeDtypeStruct((n, dff), F32)]
    else:
        in_specs += [pl.BlockSpec((1, CONV_W - 1, fc), lambda i, f: (i // tiles_per_seq, 0, f))]
        args += [conv_prev]
        out_specs = [pl.BlockSpec((tm, d), lambda i, f: (i, 0)),
                     pl.BlockSpec((1, CONV_W - 1, fc), lambda i, f: (i, 0, f))]
        out_shape = [jax.ShapeDtypeStruct((n, d), F32),
                     jax.ShapeDtypeStruct((n // tm, CONV_W - 1, dff), F32)]
        scratch += [pltpu.VMEM((nf, SUBLANE, fc), F32)]
    kern = functools.partial(_ffn_kernel, alpha=alpha, tm=tm, fc=fc, tiles_per_seq=tiles_per_seq,
                             short_seq=short_seq)
    return pl.pallas_call(
        kern, grid=(n // tm, nf), in_specs=in_specs, out_specs=out_specs, out_shape=out_shape,
        scratch_shapes=scratch, compiler_params=_cparams(("arbitrary", "arbitrary")),
    )(*args)


def _sample_attn_kernel(pt_ref,
                        qc_ref, fq_ref, kcn_ref, fkn_ref, fvn_ref, fln_ref,
                        lat_hbm, rope_hbm, fk_hbm, fv_hbm, lf_hbm,
                        ol_ref, of_ref,
                        lat_b, rope_b, fk_b, fv_b, lf_b, sems,
                        ml_m, ml_l, ml_acc, fx_m, fx_l, fx_acc, fx_p,
                        *, layer, pb, nblk, nseq, t_new):
    b = pl.program_id(0)
    j = pl.program_id(1)
    step = b * nblk + j
    slot = step % 2
    page = lat_b.shape[2]

    def copies(bb, jj, sl):
        out = []
        for p in range(pb):
            pg = pt_ref[bb, jj * pb + p]
            cols = pl.ds(p * page, page)
            out.append(pltpu.make_async_copy(lat_hbm.at[pg, layer], lat_b.at[sl, p], sems.at[sl, 0]))
            out.append(pltpu.make_async_copy(rope_hbm.at[pg, layer], rope_b.at[sl, :, cols], sems.at[sl, 1]))
            out.append(pltpu.make_async_copy(fk_hbm.at[pg, layer], fk_b.at[sl, :, cols], sems.at[sl, 2]))
            out.append(pltpu.make_async_copy(fv_hbm.at[pg, layer], fv_b.at[sl, :, cols], sems.at[sl, 3]))
            out.append(pltpu.make_async_copy(lf_hbm.at[pg, layer], lf_b.at[sl, :, cols], sems.at[sl, 4]))
        return out

    @pl.when(step == 0)
    def _():
        for c in copies(b, j, slot):
            c.start()

    nxt = step + 1
    nb = jnp.where(j == nblk - 1, b + 1, b)
    nj = jnp.where(j == nblk - 1, 0, j + 1)

    @pl.when(nxt < nseq * nblk)
    def _():
        for c in copies(nb, nj, 1 - slot):
            c.start()

    rows_m = MLA_HEADS * t_new
    rows_f = FOX_HEADS * t_new
    keys = pb * page

    @pl.when(j == 0)
    def _():
        ml_m[...] = jnp.full_like(ml_m, -jnp.inf)
        ml_l[...] = jnp.zeros_like(ml_l)
        ml_acc[...] = jnp.zeros_like(ml_acc)
        fx_m[...] = jnp.full_like(fx_m, -jnp.inf)
        fx_l[...] = jnp.zeros_like(fx_l)
        fx_acc[...] = jnp.zeros_like(fx_acc)
        fx_p[...] = jnp.zeros_like(fx_p)

    for c in copies(b, j, slot):
        c.wait()

    qc = qc_ref[...].reshape(rows_m, 2 * LANE)
    lane_f = lax.broadcasted_iota(jnp.int32, (1, FOX_WIDTH), 1)
    fq = fq_ref[...] * FOX_QSCALE
    qf = jnp.concatenate([jnp.where((lane_f // FOX_DH) == h, fq, 0.0) for h in range(FOX_HEADS)],
                         axis=0).astype(BF16)
    r8 = lax.broadcasted_iota(jnp.int32, (SUBLANE, FOX_HEADS), 0)
    l8 = lax.broadcasted_iota(jnp.int32, (SUBLANE, FOX_HEADS), 1)
    sel = jnp.where(r8 == l8, 1.0, 0.0).astype(F32)

    def lane_cumsum(x):
        n = x.shape[1]
        idx = lax.broadcasted_iota(jnp.int32, x.shape, 1)
        sh = 1
        while sh < n:
            x = x + jnp.where(idx >= sh, pltpu.roll(x, sh, axis=1), 0.0)
            sh *= 2
        return x

    def expand_rows(x8):
        return jnp.concatenate([jnp.broadcast_to(x8[h:h + 1, :], (t_new, x8.shape[1])) for h in range(FOX_HEADS)],
                               axis=0)

    def online(m_s, l_s, acc_s, s, v, v_feature_major=False):
        m_old = m_s[...]
        m_new = jnp.maximum(m_old, jnp.max(s, axis=-1, keepdims=True))
        a = jnp.exp2(m_old - m_new)
        p = jnp.exp2(s - m_new)
        l_s[...] = a * l_s[...] + jnp.sum(p, axis=-1, keepdims=True)
        p = p.astype(BF16)
        pv = _nt_dot(p, v) if v_feature_major else jnp.dot(p, v, preferred_element_type=F32)
        acc_s[...] = a * acc_s[...] + pv
        m_s[...] = m_new

    latb = lat_b[slot].reshape(keys, MLA_KV_LORA).astype(BF16)
    rope_t = rope_b[slot].astype(BF16)
    s_m = _nt_dot(qc[:, 0:LANE], latb) + jnp.dot(qc[:, LANE:LANE + MLA_ROPE], rope_t, preferred_element_type=F32)
    online(ml_m, ml_l, ml_acc, s_m, latb)

    fk_t = fk_b[slot].astype(BF16)
    fv_t = fv_b[slot].astype(BF16)
    pre = lane_cumsum(lf_b[slot]) + fx_p[...]
    fx_p[...] = pre[:, keys - 1:keys]
    s_f = jnp.dot(qf, fk_t, preferred_element_type=F32) - expand_rows(pre * LOG2E)
    online(fx_m, fx_l, fx_acc, s_f, fv_t, v_feature_major=True)

    @pl.when(j == nblk - 1)
    def _():
        kcn = kcn_ref[...]
        tq_m = lax.broadcasted_iota(jnp.int32, (MLA_HEADS, t_new, t_new), 1).reshape(rows_m, t_new)
        ts_m = lax.broadcasted_iota(jnp.int32, (rows_m, t_new), 1)
        s_n = jnp.where(ts_m <= tq_m, _nt_dot(qc, kcn), -jnp.inf)
        online(ml_m, ml_l, ml_acc, s_n, kcn[:, 0:LANE])
        o_m = ml_acc[...] / ml_l[...]
        for h in range(MLA_HEADS):
            ol_ref[:, h * LANE:(h + 1) * LANE] = o_m[h * t_new:(h + 1) * t_new, :]

        total = expand_rows(fx_p[...])
        fx_m[...] = fx_m[...] + total * LOG2E
        fcum = lane_cumsum(_nt_dot(sel, fln_ref[:, G_FOX:G_FOX + FOX_HEADS], precision=HI))
        tq_f = lax.broadcasted_iota(jnp.int32, (FOX_HEADS, t_new, t_new), 1).reshape(rows_f, t_new)
        ts_f = lax.broadcasted_iota(jnp.int32, (rows_f, t_new), 1)
        s_n = jnp.where(ts_f <= tq_f, _nt_dot(qf, fkn_ref[...]) - expand_rows(fcum * LOG2E), -jnp.inf)
        online(fx_m, fx_l, fx_acc, s_n, fvn_ref[...])
        o_f = fx_acc[...] / fx_l[...]
        out = jnp.zeros((t_new, FOX_WIDTH), F32)
        for h in range(FOX_HEADS):
            out = jnp.where((lane_f // FOX_DH) == h, o_f[h * t_new:(h + 1) * t_new, :], out)
        of_ref[...] = out


def sample_attention(page_table, qcat, proj, kcat, fkb, fvb, flog, caches, layer, nseq, t_new, pb):
    lat_c, rope_c, fk_c, fv_c, lf_c = caches
    n_pages = page_table.shape[1]
    page = lat_c.shape[2]
    nblk = n_pages // pb
    keys = pb * page
    rope_c = jnp.transpose(rope_c, (0, 1, 3, 2))
    fk_c = jnp.transpose(fk_c, (0, 1, 3, 4, 2)).reshape(fk_c.shape[0], fk_c.shape[1], FOX_WIDTH, page)
    fv_c = jnp.transpose(fv_c, (0, 1, 3, 4, 2)).reshape(fv_c.shape[0], fv_c.shape[1], FOX_WIDTH, page)
    lf_c = jnp.pad(jnp.transpose(lf_c, (0, 1, 3, 2)), ((0, 0), (0, 0), (0, SUBLANE - FOX_HEADS), (0, 0)))
    rows_m = MLA_HEADS * t_new
    rows_f = FOX_HEADS * t_new
    kern = functools.partial(_sample_attn_kernel, layer=layer, pb=pb, nblk=nblk, nseq=nseq, t_new=t_new)
    any_spec = pl.BlockSpec(memory_space=pl.ANY)
    grid_spec = pltpu.PrefetchScalarGridSpec(
        num_scalar_prefetch=1,
        grid=(nseq, nblk),
        in_specs=[pl.BlockSpec((MLA_HEADS, t_new, 2 * LANE), lambda b, j, pt: (0, b, 0)),
                  pl.BlockSpec((t_new, FOX_WIDTH), lambda b, j, pt: (b, C_FQ // FOX_WIDTH)),
                  pl.BlockSpec((t_new, 2 * LANE), lambda b, j, pt: (b, 0)),
                  pl.BlockSpec((t_new, FOX_WIDTH), lambda b, j, pt: (b, 0)),
                  pl.BlockSpec((t_new, FOX_WIDTH), lambda b, j, pt: (b, 0)),
                  pl.BlockSpec((t_new, LANE), lambda b, j, pt: (b, 0)),
                  any_spec, any_spec, any_spec, any_spec, any_spec],
        out_specs=[pl.BlockSpec((t_new, MLA_HEADS * LANE), lambda b, j, pt: (b, 0)),
                   pl.BlockSpec((t_new, FOX_WIDTH), lambda b, j, pt: (b, 0))],
        scratch_shapes=[pltpu.VMEM((2, pb, page, MLA_KV_LORA), F32),
                        pltpu.VMEM((2, MLA_ROPE, keys), F32),
                        pltpu.VMEM((2, FOX_WIDTH, keys), F32),
                        pltpu.VMEM((2, FOX_WIDTH, keys), F32),
                        pltpu.VMEM((2, SUBLANE, keys), F32),
                        pltpu.SemaphoreType.DMA((2, 5)),
                        pltpu.VMEM((rows_m, 1), F32), pltpu.VMEM((rows_m, 1), F32),
                        pltpu.VMEM((rows_m, LANE), F32),
                        pltpu.VMEM((rows_f, 1), F32), pltpu.VMEM((rows_f, 1), F32),
                        pltpu.VMEM((rows_f, FOX_WIDTH), F32),
                        pltpu.VMEM((SUBLANE, 1), F32)],
    )
    n = nseq * t_new
    return pl.pallas_call(
        kern,
        grid_spec=grid_spec,
        out_shape=[jax.ShapeDtypeStruct((n, MLA_HEADS * LANE), F32),
                   jax.ShapeDtypeStruct((n, FOX_WIDTH), F32)],
        compiler_params=_cparams(("arbitrary", "arbitrary")),
    )(page_table, qcat, proj, kcat, fkb, fvb, flog, lat_c, rope_c, fk_c, fv_c, lf_c)


def _pad_cols(w, width):
    return jnp.pad(w, ((0, 0), (0, width - w.shape[1])))


def _prep_layer_weights(w_in, mla_w_uq, mla_w_uk, mla_w_uv):
    d = w_in.shape[0]
    splits = (MLA_Q_LORA, MLA_KV_LORA, MLA_ROPE, FOX_WIDTH, FOX_WIDTH, FOX_WIDTH, FOX_HEADS,
              ML_QW, ML_QW, ML_VW, ML_HEADS, ML_HEADS, ML_VW)
    offs = [0]
    for s in splits:
        offs.append(offs[-1] + s)
    (cq, ckv, kr, fq, fk, fv, ff, mq, mk, mv, mi, mf, mo) = [w_in[:, offs[i]:offs[i + 1]] for i in range(len(splits))]
    half = MLA_ROPE // 2
    swap = lambda w: jnp.concatenate([w[..., half:], w[..., :half]], axis=-1)
    kr_blk = _pad_cols(jnp.concatenate([kr, swap(kr)], axis=1), LANE)
    gate_blk = jnp.concatenate([_pad_cols(ff, G_MI - G_FOX), _pad_cols(mi, G_MF - G_MI), _pad_cols(mf, LANE - G_MF)],
                               axis=1)
    w_ext = jnp.concatenate([cq, ckv, kr_blk, fq, fk, fv, _pad_cols(mq, PAD_Q), _pad_cols(mk, PAD_Q), gate_blk,
                             mv, mo], axis=1).astype(BF16)
    assert w_ext.shape == (d, D_EXT)
    uq = mla_w_uq.reshape(MLA_Q_LORA, MLA_HEADS, MLA_NOPE + MLA_ROPE)
    wn = uq[:, :, :MLA_NOPE].reshape(MLA_Q_LORA, MLA_HEADS * MLA_NOPE)
    rope = uq[:, :, MLA_NOPE:]
    pad_r = lambda w: jnp.pad(w, ((0, 0), (0, 0), (0, LANE - MLA_ROPE))).reshape(MLA_Q_LORA, MLA_HEADS * LANE)
    wr = pad_r(rope)
    wrs = pad_r(swap(rope))
    eye = jnp.eye(MLA_HEADS, dtype=F32)
    uk = jnp.transpose(mla_w_uk, (1, 2, 0))
    wuk_bd = (uk[:, :, None, :] * eye[:, None, :, None]).reshape(MLA_HEADS * MLA_NOPE, MLA_HEADS * MLA_KV_LORA)
    uv = jnp.transpose(mla_w_uv, (1, 0, 2))
    wuv_bd = (uv[:, :, None, :] * eye[:, None, :, None]).reshape(MLA_HEADS * MLA_KV_LORA, MLA_HEADS * MLA_V)
    return w_ext, wn.astype(BF16), wr.astype(BF16), wrs.astype(BF16), wuk_bd.astype(BF16), wuv_bd.astype(BF16)


def _rope_tables(pos, reps):
    inv_freq = 1.0 / (ROPE_THETA ** (jnp.arange(0, MLA_ROPE, 2, dtype=F32) / MLA_ROPE))
    ang = pos.astype(F32)[:, None] * inv_freq[None, :]
    cos, sin = jnp.cos(ang), jnp.sin(ang)
    cos_t = _pad_cols(jnp.concatenate([cos, cos], axis=1), LANE)
    sin_t = _pad_cols(jnp.concatenate([-sin, sin], axis=1), LANE)
    return jnp.tile(cos_t, (reps, 1)), jnp.tile(sin_t, (reps, 1))


def _gate_row(vec, off):
    return jnp.zeros((1, LANE), F32).at[0, off:off + vec.shape[0]].set(vec)


def _state_to_blockdiag(c, n, m):
    bsz = c.shape[0]
    eye = jnp.eye(ML_HEADS, dtype=F32)
    cbd = (c[:, :, :, None, :] * eye[None, :, None, :, None]).reshape(bsz, ML_QW, ML_VW)
    cbd = jnp.pad(cbd, ((0, 0), (0, PAD_Q - ML_QW), (0, 0)))
    nrow = jnp.pad(n.reshape(bsz, 1, ML_QW), ((0, 0), (0, 0), (0, PAD_Q - ML_QW)))
    mrow = jnp.pad(m.reshape(bsz, 1, ML_HEADS), ((0, 0), (0, 0), (0, LANE - ML_HEADS)))
    return cbd, nrow, mrow


def _blockdiag_to_state(cbd, nrow, mrow):
    bsz = cbd.shape[0]
    c5 = cbd[:, :ML_QW, :].reshape(bsz, ML_HEADS, ML_DK, ML_HEADS, ML_DV)
    c = jnp.stack([c5[:, h, :, h, :] for h in range(ML_HEADS)], axis=1)
    n = nrow[:, 0, :ML_QW].reshape(bsz, ML_HEADS, ML_DK)
    m = mrow[:, 0, :ML_HEADS]
    return c, n, m


def _run_group(x, pe, lw, *, batch, seq, pos, alpha, tm, past):
    n = x.shape[0]
    w_ext, wn, wr, wrs, wuk_bd, wuv_bd = lw['prepped']
    proj = in_projection(x, w_ext, tm)
    reps = 1 if past is None else n // seq
    cos_t, sin_t = _rope_tables(pos, reps)
    lat, kpe, kcat, qcat = mla_prep(proj, cos_t, sin_t, lw['mla_q_norm'][None, :], lw['mla_kv_norm'][None, :],
                                    wn, wr, wrs, wuk_bd, tm)
    fox_bias = _gate_row(lw['fox_f_bias'], G_FOX)
    kbf, vbf, flog, fcum = fox_prep(proj, fox_bias, seq, tm)
    ib_row = _gate_row(lw['mlstm_i_bias'], G_MI)
    fb_row = _gate_row(lw['mlstm_f_bias'], G_MF)
    gn_row = lw['mlstm_norm'].reshape(1, ML_VW)
    if past is None:
        olat = mla_flash(qcat, kcat, batch, seq, tq=256, tk=512)
        fcum_t = jnp.transpose(fcum.reshape(batch, seq, LANE)[:, :, :SUBLANE], (0, 2, 1))
        ofox = fox_flash(proj, kbf, vbf, fcum_t, batch, seq, tq=256, tk=512)
        c0 = jnp.zeros((batch, PAD_Q, ML_VW), F32)
        n0 = jnp.zeros((batch, 1, PAD_Q), F32)
        m0 = jnp.zeros((batch, 1, LANE), F32)
        chunk = ML_CHUNK if seq % ML_CHUNK == 0 else seq
    else:
        olat, ofox = sample_attention(past['page_table'], qcat, proj, kcat, kbf, vbf, flog, past['caches'],
                                      past['layer'], batch, seq, pb=min(32, past['page_table'].shape[1]))
        c0, n0, m0 = _state_to_blockdiag(past['c'], past['n'], past['m'])
        chunk = ML_CHUNK if seq % ML_CHUNK == 0 else seq
    seqs_per_step = next(d for d in (4, 2, 1) if batch % d == 0)
    oml, c1, n1, m1 = mlstm(proj, ib_row, fb_row, gn_row, c0, n0, m0, batch, seq, chunk, seqs_per_step)
    c1, n1, m1 = _blockdiag_to_state(c1, n1, m1)
    x1 = out_projection(olat, ofox, oml, x, wuv_bd, lw['w_out'], lw['ln1_g'][None, :], lw['ln1_b'][None, :], alpha, tm)
    dff = lw['ffn_w_down'].shape[0]
    ffn_args = (x1, lw['ffn_w_up'], lw['ffn_conv_w'], lw['ffn_conv_b'][None, :], lw['ffn_w_down'],
                lw['ln2_g'][None, :], lw['ln2_b'][None, :], pe, lw['ple_w_gate'], lw['ple_w_proj'], alpha, tm,
                _ffn_chunk(dff, wide=past is None))
    if past is None:
        x2, tails = conv_ffn(*ffn_args, seq_len=seq, conv_prev=jnp.zeros((batch, CONV_W - 1, dff), F32))
        conv_new = tails.reshape(batch, seq // tm, CONV_W - 1, dff)[:, -1]
    else:
        prev = past['conv']
        e1 = jnp.pad(prev[:, 1:2], ((0, 0), (0, seq - 1), (0, 0))).reshape(n, dff)
        e2 = jnp.pad(prev, ((0, 0), (0, seq - 2), (0, 0))).reshape(n, dff)
        x2, gate = conv_ffn(*ffn_args, seq_len=seq, e1=e1, e2=e2)
        conv_new = gate.reshape(batch, seq, dff)[:, seq - (CONV_W - 1):]
    d_lat = lat.reshape(batch, seq, MLA_KV_LORA)
    d_rope = kpe[:, :MLA_ROPE].reshape(batch, seq, MLA_ROPE)
    d_fk = proj[:, C_FK:C_FK + FOX_WIDTH].reshape(batch, seq, FOX_HEADS, FOX_DH)
    d_fv = proj[:, C_FV:C_FV + FOX_WIDTH].reshape(batch, seq, FOX_HEADS, FOX_DH)
    d_lf = flog[:, G_FOX:G_FOX + FOX_HEADS].reshape(batch, seq, FOX_HEADS)
    return x2, (d_lat, d_rope, d_fk, d_fv, d_lf, c1, n1, m1, conv_new)


def kernel(x_prompt, x_sample, cache_mla_latent, cache_mla_rope, cache_fox_k, cache_fox_v, cache_fox_logf,
           state_mlstm_c, state_mlstm_n, state_mlstm_m, state_ffn_conv, page_table, p_prompt, p_sample,
           w_in, mla_q_norm, mla_w_uq, mla_kv_norm, mla_w_uk, mla_w_uv, fox_f_bias, mlstm_i_bias, mlstm_f_bias,
           mlstm_norm, w_out, ln1_g, ln1_b, ffn_w_up, ffn_conv_w, ffn_conv_b, ffn_w_down, ln2_g, ln2_b,
           ple_w_gate, ple_w_proj):
    depth = w_in.shape[0]
    bp, sp, d = x_prompt.shape
    bs, ss, _ = x_sample.shape
    past_len = page_table.shape[1] * cache_mla_latent.shape[2]
    alpha = (2 * depth) ** 0.25
    pos_p = jnp.arange(sp)
    pos_s = past_len + jnp.arange(ss)
    xp = x_prompt.reshape(bp * sp, d)
    xs = x_sample.reshape(bs * ss, d)
    caches = (cache_mla_latent, cache_mla_rope, cache_fox_k, cache_fox_v, cache_fox_logf)
    per_p, per_s = [], []
    for i in range(depth):
        lw = {'prepped': _prep_layer_weights(w_in[i], mla_w_uq[i], mla_w_uk[i], mla_w_uv[i]),
              'mla_q_norm': mla_q_norm[i], 'mla_kv_norm': mla_kv_norm[i], 'fox_f_bias': fox_f_bias[i],
              'mlstm_i_bias': mlstm_i_bias[i], 'mlstm_f_bias': mlstm_f_bias[i], 'mlstm_norm': mlstm_norm[i],
              'w_out': w_out[i].astype(BF16), 'ln1_g': ln1_g[i], 'ln1_b': ln1_b[i],
              'ffn_w_up': ffn_w_up[i].astype(BF16), 'ffn_conv_w': ffn_conv_w[i], 'ffn_conv_b': ffn_conv_b[i],
              'ffn_w_down': ffn_w_down[i].astype(BF16), 'ln2_g': ln2_g[i], 'ln2_b': ln2_b[i],
              'ple_w_gate': ple_w_gate[i].astype(BF16), 'ple_w_proj': ple_w_proj[i].astype(BF16)}
        xp, st_p = _run_group(xp, p_prompt[i].reshape(bp * sp, -1), lw, batch=bp, seq=sp, pos=pos_p, alpha=alpha,
                              tm=min(512, bp * sp), past=None)
        per_p.append(st_p)
        past = {'page_table': page_table, 'caches': caches, 'layer': i, 'c': state_mlstm_c[i],
                'n': state_mlstm_n[i], 'm': state_mlstm_m[i], 'conv': state_ffn_conv[i]}
        xs, st_s = _run_group(xs, p_sample[i].reshape(bs * ss, -1), lw, batch=bs, seq=ss, pos=pos_s, alpha=alpha,
                              tm=min(512, bs * ss), past=past)
        per_s.append(st_s)

    def stack(per_layer):
        cols = list(zip(*per_layer))
        return [jnp.stack(c, axis=1 if k < 5 else 0) for k, c in enumerate(cols)]

    return tuple([xp.reshape(bp, sp, d), xs.reshape(bs, ss, d)] + stack(per_p) + stack(per_s))
```

```python
import functools
import math

import jax
import jax.numpy as jnp
from jax import lax
from jax.experimental import pallas as pl
from jax.experimental.pallas import tpu as pltpu

F32 = jnp.float32
BF16 = jnp.bfloat16
HI = lax.Precision.HIGHEST

MLA_HEADS = 6
MLA_NOPE = 64
MLA_ROPE = 32
MLA_V = 64
MLA_Q_LORA = 256
MLA_KV_LORA = 128
ROPE_THETA = 10000.0
MLA_SCALE = (MLA_NOPE + MLA_ROPE) ** -0.5
FOX_HEADS = 4
FOX_DH = 64
FOX_WIDTH = FOX_HEADS * FOX_DH
FOX_SCALE = FOX_DH ** -0.5
LOG2E = math.log2(math.e)
MLA_QSCALE = MLA_SCALE * LOG2E
FOX_QSCALE = FOX_SCALE * LOG2E
ML_HEADS = 6
ML_DK = 32
ML_DV = 64
ML_QW = ML_HEADS * ML_DK
ML_VW = ML_HEADS * ML_DV
ML_CHUNK = 128
CONV_W = 3
LN_EPS = 1e-5
RMS_EPS = 1e-6

LANE = 128
SUBLANE = 8
VMEM_LIMIT = 56 * 1024 * 1024
N_SLOTS = 3

C_CQ = 0
C_CKV = 256
C_KR = 384
C_FQ = 512
C_FK = 768
C_FV = 1024
C_MQ = 1280
C_MK = 1536
C_G = 1792
C_MV = 1920
C_MO = 2304
D_EXT = 2688
G_FOX = 0
G_MI = 8
G_MF = 16
PAD_Q = 256


def _cparams(sem):
    return pltpu.CompilerParams(dimension_semantics=sem, vmem_limit_bytes=VMEM_LIMIT)


def _log_sigmoid(x):
    return jnp.minimum(x, 0.0) - jnp.log1p(jnp.exp(-jnp.abs(x)))


def _nt_dot(a, b, precision=None):
    return lax.dot_general(a, b, (((1,), (1,)), ((), ())), preferred_element_type=F32, precision=precision)


def _mm_kernel(x_ref, w_ref, o_ref):
    o_ref[...] = jnp.dot(x_ref[...].astype(BF16), w_ref[...], preferred_element_type=F32)


def in_projection(x, w_ext, tm):
    n, d = x.shape
    dout = w_ext.shape[1]
    return pl.pallas_call(
        _mm_kernel,
        grid=(n // tm,),
        in_specs=[pl.BlockSpec((tm, d), lambda i: (i, 0)),
                  pl.BlockSpec((d, dout), lambda i: (0, 0))],
        out_specs=pl.BlockSpec((tm, dout), lambda i: (i, 0)),
        out_shape=jax.ShapeDtypeStruct((n, dout), F32),
        compiler_params=_cparams(("arbitrary",)),
    )(x, w_ext)


def _mla_prep_kernel(p_ref, cos_ref, sin_ref, gq_ref, gkv_ref, wn_ref, wr_ref, wrs_ref, wuk_ref,
                     lat_ref, kpe_ref, kcat_ref, qcat_ref):
    cq = p_ref[:, C_CQ:C_CQ + MLA_Q_LORA]
    ckv = p_ref[:, C_CKV:C_CKV + MLA_KV_LORA]
    krb = p_ref[:, C_KR:C_KR + LANE]
    cos_t = cos_ref[...]
    sin_t = sin_ref[...]
    qn = (cq * lax.rsqrt(jnp.mean(cq * cq, axis=-1, keepdims=True) + RMS_EPS) * gq_ref[...]).astype(BF16)
    lat = ckv * lax.rsqrt(jnp.mean(ckv * ckv, axis=-1, keepdims=True) + RMS_EPS) * gkv_ref[...]
    kpe = krb * cos_t + pltpu.roll(krb, LANE - MLA_ROPE, axis=1) * sin_t
    lat_ref[...] = lat
    kpe_ref[...] = kpe
    kcat_ref[:, 0:LANE] = lat.astype(BF16)
    kcat_ref[:, LANE:2 * LANE] = kpe.astype(BF16)
    qnope = jnp.dot(qn, wn_ref[...], preferred_element_type=F32).astype(BF16)
    qabs = jnp.dot(qnope, wuk_ref[...], preferred_element_type=F32)
    qr = jnp.dot(qn, wr_ref[...], preferred_element_type=F32)
    qrs = jnp.dot(qn, wrs_ref[...], preferred_element_type=F32)
    for h in range(MLA_HEADS):
        sl = slice(h * LANE, (h + 1) * LANE)
        qpe = qr[:, sl] * cos_t + qrs[:, sl] * sin_t
        qcat_ref[h, :, 0:LANE] = (qabs[:, sl] * MLA_QSCALE).astype(BF16)
        qcat_ref[h, :, LANE:2 * LANE] = (qpe * MLA_QSCALE).astype(BF16)


def mla_prep(proj, cos_t, sin_t, gq, gkv, wn, wr, wrs, wuk, tm):
    n = proj.shape[0]
    t_tiles = cos_t.shape[0] // tm
    const = lambda i: (0, 0)
    return pl.pallas_call(
        _mla_prep_kernel,
        grid=(n // tm,),
        in_specs=[pl.BlockSpec((tm, 512), lambda i: (i, 0)),
                  pl.BlockSpec((tm, LANE), lambda i: (i % t_tiles, 0)),
                  pl.BlockSpec((tm, LANE), lambda i: (i % t_tiles, 0)),
                  pl.BlockSpec(gq.shape, const), pl.BlockSpec(gkv.shape, const),
                  pl.BlockSpec(wn.shape, const), pl.BlockSpec(wr.shape, const),
                  pl.BlockSpec(wrs.shape, const), pl.BlockSpec(wuk.shape, const)],
        out_specs=[pl.BlockSpec((tm, LANE), lambda i: (i, 0)),
                   pl.BlockSpec((tm, LANE), lambda i: (i, 0)),
                   pl.BlockSpec((tm, 2 * LANE), lambda i: (i, 0)),
                   pl.BlockSpec((MLA_HEADS, tm, 2 * LANE), lambda i: (0, i, 0))],
        out_shape=[jax.ShapeDtypeStruct((n, LANE), F32),
                   jax.ShapeDtypeStruct((n, LANE), F32),
                   jax.ShapeDtypeStruct((n, 2 * LANE), BF16),
                   jax.ShapeDtypeStruct((MLA_HEADS, n, 2 * LANE), BF16)],
        compiler_params=_cparams(("arbitrary",)),
    )(proj, cos_t, sin_t, gq, gkv, wn, wr, wrs, wuk)


def _fox_prep_kernel(k_ref, v_ref, g_ref, b_ref, kb_ref, vb_ref, fl_ref, fc_ref, kf_ref, vf_ref, carry, *,
                     tiles_per_seq, seg):
    i = pl.program_id(0)
    tm = g_ref.shape[0]
    kb_ref[...] = k_ref[...].astype(BF16)
    vb_ref[...] = v_ref[...].astype(BF16)
    kf_ref[...] = k_ref[...]
    vf_ref[...] = v_ref[...]
    fl = _log_sigmoid(g_ref[...] + b_ref[...])
    fl_ref[...] = fl
    row = lax.broadcasted_iota(jnp.int32, (tm, tm), 0)
    col = lax.broadcasted_iota(jnp.int32, (tm, tm), 1)
    keep = col <= row
    if seg < tm:
        keep = jnp.logical_and(keep, (row // seg) == (col // seg))
    tri = jnp.where(keep, 1.0, 0.0).astype(F32)
    csum = jnp.dot(tri, fl, preferred_element_type=F32, precision=HI)

    @pl.when(i % tiles_per_seq == 0)
    def _():
        carry[...] = jnp.zeros_like(carry)

    csum = csum + carry[...]
    fc_ref[...] = csum
    carry[...] = csum[tm - 1:tm, :]


def fox_prep(proj, bias_row, seq_len, tm):
    n = proj.shape[0]
    tiles_per_seq = max(seq_len // tm, 1)
    kern = functools.partial(_fox_prep_kernel, tiles_per_seq=tiles_per_seq, seg=seq_len)
    return pl.pallas_call(
        kern,
        grid=(n // tm,),
        in_specs=[pl.BlockSpec((tm, FOX_WIDTH), lambda i: (i, C_FK // FOX_WIDTH)),
                  pl.BlockSpec((tm, FOX_WIDTH), lambda i: (i, C_FV // FOX_WIDTH)),
                  pl.BlockSpec((tm, LANE), lambda i: (i, C_G // LANE)),
                  pl.BlockSpec((1, LANE), lambda i: (0, 0))],
        out_specs=[pl.BlockSpec((tm, FOX_WIDTH), lambda i: (i, 0)),
                   pl.BlockSpec((tm, FOX_WIDTH), lambda i: (i, 0)),
                   pl.BlockSpec((tm, LANE), lambda i: (i, 0)),
                   pl.BlockSpec((tm, LANE), lambda i: (i, 0)),
                   pl.BlockSpec((tm, FOX_WIDTH), lambda i: (i, 0)),
                   pl.BlockSpec((tm, FOX_WIDTH), lambda i: (i, 0))],
        out_shape=[jax.ShapeDtypeStruct((n, FOX_WIDTH), BF16),
                   jax.ShapeDtypeStruct((n, FOX_WIDTH), BF16),
                   jax.ShapeDtypeStruct((n, LANE), F32),
                   jax.ShapeDtypeStruct((n, LANE), F32),
                   jax.ShapeDtypeStruct((n, FOX_WIDTH), F32),
                   jax.ShapeDtypeStruct((n, FOX_WIDTH), F32)],
        scratch_shapes=[pltpu.VMEM((1, LANE), F32)],
        compiler_params=_cparams(("arbitrary",)),
    )(proj, proj, proj, bias_row)


def _softmax_step(s, v, m_s, l_s, acc_s):
    tk = s.shape[1]
    m_old = m_s[...]
    m_new = jnp.maximum(m_old, jnp.max(s, axis=-1, keepdims=True))
    a = jnp.exp2(m_old - m_new)
    lsum = a * l_s[...]
    ps = []
    for c in range(tk // LANE):
        p_c = jnp.exp2(s[:, c * LANE:(c + 1) * LANE] - m_new)
        lsum = lsum + p_c
        ps.append(p_c.astype(BF16))
    p = jnp.concatenate(ps, axis=1) if len(ps) > 1 else ps[0]
    pv = jnp.dot(p, v, preferred_element_type=F32)
    dv = pv.shape[1]
    a_full = a if dv == LANE else jnp.concatenate([a] * (dv // LANE), axis=1)
    acc_s[...] = a_full * acc_s[...] + pv
    l_s[...] = lsum
    m_s[...] = m_new


def _causal_loops(qi, tq, tk, step):
    n_full = (qi * tq) // tk
    n_end = ((qi + 1) * tq + tk - 1) // tk

    def full_body(j, c):
        step(j, False)
        return c

    def diag_body(j, c):
        step(j, True)
        return c

    lax.fori_loop(0, n_full, full_body, 0)
    lax.fori_loop(n_full, n_end, diag_body, 0)


def _mla_flash_kernel(q_ref, k_ref, o_ref, m_s, l_s, acc_s, *, tq, tk):
    qi = pl.program_id(1)
    rows = MLA_HEADS * tq
    q = q_ref[...].reshape(rows, 2 * LANE)
    m_s[...] = jnp.full_like(m_s, -jnp.inf)
    l_s[...] = jnp.zeros_like(l_s)
    acc_s[...] = jnp.zeros_like(acc_s)

    def step(j, masked):
        kb = k_ref[pl.ds(pl.multiple_of(j * tk, tk), tk), :]
        s = _nt_dot(q, kb)
        if masked:
            qpos = qi * tq + lax.broadcasted_iota(jnp.int32, (MLA_HEADS, tq, tk), 1).reshape(rows, tk)
            kpos = j * tk + lax.broadcasted_iota(jnp.int32, (rows, tk), 1)
            s = jnp.where(kpos <= qpos, s, -jnp.inf)
        _softmax_step(s, kb[:, 0:LANE], m_s, l_s, acc_s)

    _causal_loops(qi, tq, tk, step)
    o = acc_s[...] / jnp.sum(l_s[...], axis=-1, keepdims=True)
    for h in range(MLA_HEADS):
        o_ref[:, h * LANE:(h + 1) * LANE] = o[h * tq:(h + 1) * tq, :]


def mla_flash(qcat, kcat, batch, seq, tq, tk):
    n = kcat.shape[0]
    nq = seq // tq
    kern = functools.partial(_mla_flash_kernel, tq=tq, tk=tk)
    rows = MLA_HEADS * tq
    return pl.pallas_call(
        kern,
        grid=(batch, nq),
        in_specs=[pl.BlockSpec((MLA_HEADS, tq, 2 * LANE), lambda b, i: (0, b * nq + i, 0)),
                  pl.BlockSpec((seq, 2 * LANE), lambda b, i: (b, 0))],
        out_specs=pl.BlockSpec((tq, MLA_HEADS * LANE), lambda b, i: (b * nq + i, 0)),
        out_shape=jax.ShapeDtypeStruct((n, MLA_HEADS * LANE), F32),
        scratch_shapes=[pltpu.VMEM((rows, LANE), F32), pltpu.VMEM((rows, LANE), F32), pltpu.VMEM((rows, LANE), F32)],
        compiler_params=_cparams(("arbitrary", "arbitrary")),
    )(qcat, kcat)


def _fox_flash_kernel(q_ref, k_ref, v_ref, ft_ref, o_ref, m_s, l_s, acc_s, *, tq, tk):
    qi = pl.program_id(1)
    rows = FOX_HEADS * tq
    q = q_ref[...] * FOX_QSCALE
    lane = lax.broadcasted_iota(jnp.int32, (1, FOX_WIDTH), 1)
    qs = jnp.concatenate([jnp.where((lane // FOX_DH) == h, q, 0.0) for h in range(FOX_HEADS)],
                         axis=0).astype(BF16)
    m_s[...] = jnp.full_like(m_s, -jnp.inf)
    l_s[...] = jnp.zeros_like(l_s)
    acc_s[...] = jnp.zeros_like(acc_s)

    def step(j, masked):
        start = pl.multiple_of(j * tk, tk)
        kb = k_ref[pl.ds(start, tk), :]
        vb = v_ref[pl.ds(start, tk), :]
        qk = _nt_dot(qs, kb)
        s = jnp.concatenate([qk[h * tq:(h + 1) * tq, :] - ft_ref[0, h:h + 1, pl.ds(start, tk)] * LOG2E
                             for h in range(FOX_HEADS)], axis=0)
        if masked:
            qpos = qi * tq + lax.broadcasted_iota(jnp.int32, (FOX_HEADS, tq, tk), 1).reshape(rows, tk)
            kpos = j * tk + lax.broadcasted_iota(jnp.int32, (rows, tk), 1)
            s = jnp.where(kpos <= qpos, s, -jnp.inf)
        _softmax_step(s, vb, m_s, l_s, acc_s)

    _causal_loops(qi, tq, tk, step)
    o = acc_s[...] / jnp.sum(l_s[...], axis=-1, keepdims=True)
    out = o[0:tq, :]
    for h in range(1, FOX_HEADS):
        out = jnp.where((lane // FOX_DH) == h, o[h * tq:(h + 1) * tq, :], out)
    o_ref[...] = out


def fox_flash(proj, kbf, vbf, fcum_t, batch, seq, tq, tk):
    n = proj.shape[0]
    nq = seq // tq
    rows = FOX_HEADS * tq
    kern = functools.partial(_fox_flash_kernel, tq=tq, tk=tk)
    return pl.pallas_call(
        kern,
        grid=(batch, nq),
        in_specs=[pl.BlockSpec((tq, FOX_WIDTH), lambda b, i: (b * nq + i, C_FQ // FOX_WIDTH)),
                  pl.BlockSpec((seq, FOX_WIDTH), lambda b, i: (b, 0)),
                  pl.BlockSpec((seq, FOX_WIDTH), lambda b, i: (b, 0)),
                  pl.BlockSpec((1, SUBLANE, seq), lambda b, i: (b, 0, 0))],
        out_specs=pl.BlockSpec((tq, FOX_WIDTH), lambda b, i: (b * nq + i, 0)),
        out_shape=jax.ShapeDtypeStruct((n, FOX_WIDTH), F32),
        scratch_shapes=[pltpu.VMEM((rows, LANE), F32), pltpu.VMEM((rows, LANE), F32),
                        pltpu.VMEM((rows, FOX_WIDTH), F32)],
        compiler_params=_cparams(("arbitrary", "arbitrary")),
    )(proj, kbf, vbf, fcum_t)


def _mlstm_chunk(q, k, v, g, mo, ib, fb, gn, c_old, n_old, m_old, L):
    k = k * (ML_DK ** -0.5)
    gi = g + ib
    lf = _log_sigmoid(g + fb)
    row = lax.broadcasted_iota(jnp.int32, (L, L), 0)
    col = lax.broadcasted_iota(jnp.int32, (L, L), 1)
    causal = col <= row
    tri = jnp.where(causal, 1.0, 0.0).astype(F32)
    b = jnp.dot(tri, lf, preferred_element_type=F32, precision=HI)
    r8 = lax.broadcasted_iota(jnp.int32, (SUBLANE, LANE), 0)
    l8 = lax.broadcasted_iota(jnp.int32, (SUBLANE, LANE), 1)
    sel_f = jnp.where(l8 == r8 + G_MF, 1.0, 0.0).astype(F32)
    sel_i = jnp.where(l8 == r8 + G_MI, 1.0, 0.0).astype(F32)
    b_rows = _nt_dot(sel_f, b, precision=HI)
    i_rows = _nt_dot(sel_i, gi, precision=HI)

    qb = q.astype(BF16)
    kb = k.astype(BF16)
    vb = v.astype(BF16)
    qc_all = jnp.dot(qb, c_old.astype(BF16), preferred_element_type=F32)
    qn_all = q * n_old
    lane_q = lax.broadcasted_iota(jnp.int32, (1, PAD_Q), 1)
    lane_v = lax.broadcasted_iota(jnp.int32, (1, ML_VW), 1)
    lane_m = lax.broadcasted_iota(jnp.int32, (1, LANE), 1)
    b_last = b[L - 1:L, :]

    h_all = jnp.zeros((L, ML_VW), F32)
    kws = jnp.zeros((L, PAD_Q), F32)
    wc_v = jnp.zeros((1, ML_VW), F32)
    wc_q = jnp.zeros((1, PAD_Q), F32)
    m_new_row = m_old
    for h in range(ML_HEADS):
        mq = (lane_q // ML_DK) == h
        mv = (lane_v // ML_DV) == h
        bcol = b[:, G_MF + h:G_MF + h + 1]
        icol = gi[:, G_MI + h:G_MI + h + 1]
        brow = b_rows[h:h + 1, :]
        irow = i_rows[h:h + 1, :]
        m_h = m_old[:, h:h + 1]
        d = jnp.where(causal, bcol - brow + irow, -jnp.inf)
        inter = bcol + m_h
        mt = jnp.maximum(inter, jnp.max(d, axis=-1, keepdims=True))
        w_inter = jnp.exp(inter - mt)
        s = _nt_dot(jnp.where(mq, q, 0.0).astype(BF16), kb)
        a = jnp.exp(d - mt) * s
        av = jnp.dot(a.astype(BF16), vb, preferred_element_type=F32)
        qn = jnp.sum(jnp.where(mq, qn_all, 0.0), axis=-1, keepdims=True)
        den = w_inter * qn + jnp.sum(a, axis=-1, keepdims=True)
        hden = jnp.maximum(jnp.abs(den), jnp.exp(-mt))
        h_all = jnp.where(mv, (w_inter * qc_all + av) / hden, h_all)
        bl = b_last[:, G_MF + h:G_MF + h + 1]
        gg = bl - bcol + icol
        m_new = jnp.maximum(bl + m_h, jnp.max(gg, axis=0, keepdims=True))
        wc = jnp.exp(bl + m_h - m_new)
        ws = jnp.exp(gg - m_new)
        kws = jnp.where(mq, k * ws, kws)
        wc_v = jnp.where(mv, wc, wc_v)
        wc_q = jnp.where(mq, wc, wc_q)
        m_new_row = jnp.where(lane_m == h, m_new, m_new_row)

    kv = lax.dot_general(kws.astype(BF16), vb, (((0,), (0,)), ((), ())), preferred_element_type=F32)
    r_c = lax.broadcasted_iota(jnp.int32, (PAD_Q, ML_VW), 0)
    l_c = lax.broadcasted_iota(jnp.int32, (PAD_Q, ML_VW), 1)
    bd = (r_c // ML_DK) == (l_c // ML_DV)
    c_new = c_old * wc_v + jnp.where(bd, kv, 0.0)
    n_new = n_old * wc_q + jnp.sum(kws, axis=0, keepdims=True)

    r_g = lax.broadcasted_iota(jnp.int32, (ML_VW, ML_VW), 0)
    l_g = lax.broadcasted_iota(jnp.int32, (ML_VW, ML_VW), 1)
    grp = jnp.where((r_g // ML_DV) == (l_g // ML_DV), 1.0 / ML_DV, 0.0).astype(BF16)
    h2 = h_all * h_all
    h2_hi = h2.astype(BF16)
    h2_lo = (h2 - h2_hi.astype(F32)).astype(BF16)
    ms = jnp.dot(h2_hi, grp, preferred_element_type=F32) + jnp.dot(h2_lo, grp, preferred_element_type=F32)
    hn = h_all * lax.rsqrt(ms + RMS_EPS) * gn
    return hn * jax.nn.sigmoid(mo), c_new, n_new, m_new_row


def _mlstm_kernel(q_ref, k_ref, v_ref, g_ref, mo_ref, ib_ref, fb_ref, gn_ref, c0_ref, n0_ref, m0_ref,
                  h_ref, c_ref, n_ref, m_ref, c_s, n_s, m_s, *, L, nb):
    j = pl.program_id(1)

    @pl.when(j == 0)
    def _():
        c_s[...] = c0_ref[...]
        n_s[...] = n0_ref[...]
        m_s[...] = m0_ref[...]

    for bi in range(nb):
        h, c_new, n_new, m_new = _mlstm_chunk(q_ref[bi], k_ref[bi], v_ref[bi], g_ref[bi], mo_ref[bi], ib_ref[...],
                                              fb_ref[...], gn_ref[...], c_s[bi], n_s[bi], m_s[bi], L)
        h_ref[bi] = h
        c_s[bi] = c_new
        n_s[bi] = n_new
        m_s[bi] = m_new
        c_ref[bi] = c_new
        n_ref[bi] = n_new
        m_ref[bi] = m_new


def mlstm(proj, ib_row, fb_row, gn_row, c0, n0, m0, batch, seq, L, nb):
    n, dext = proj.shape
    nc = seq // L
    proj3 = proj.reshape(batch, seq, dext)
    kern = functools.partial(_mlstm_kernel, L=L, nb=nb)
    const = lambda b, j: (0, 0)
    st = lambda b, j: (b, 0, 0)
    col = lambda c: (lambda b, j: (b, j, c))
    h, c1, n1, m1 = pl.pallas_call(
        kern,
        grid=(batch // nb, nc),
        in_specs=[pl.BlockSpec((nb, L, PAD_Q), col(C_MQ // PAD_Q)),
                  pl.BlockSpec((nb, L, PAD_Q), col(C_MK // PAD_Q)),
                  pl.BlockSpec((nb, L, ML_VW), col(C_MV // ML_VW)),
                  pl.BlockSpec((nb, L, LANE), col(C_G // LANE)),
                  pl.BlockSpec((nb, L, ML_VW), col(C_MO // ML_VW)),
                  pl.BlockSpec((1, LANE), const), pl.BlockSpec((1, LANE), const),
                  pl.BlockSpec((1, ML_VW), const),
                  pl.BlockSpec((nb, PAD_Q, ML_VW), st), pl.BlockSpec((nb, 1, PAD_Q), st),
                  pl.BlockSpec((nb, 1, LANE), st)],
        out_specs=[pl.BlockSpec((nb, L, ML_VW), lambda b, j: (b, j, 0)),
                   pl.BlockSpec((nb, PAD_Q, ML_VW), st), pl.BlockSpec((nb, 1, PAD_Q), st),
                   pl.BlockSpec((nb, 1, LANE), st)],
        out_shape=[jax.ShapeDtypeStruct((batch, seq, ML_VW), F32),
                   jax.ShapeDtypeStruct((batch, PAD_Q, ML_VW), F32),
                   jax.ShapeDtypeStruct((batch, 1, PAD_Q), F32),
                   jax.ShapeDtypeStruct((batch, 1, LANE), F32)],
        scratch_shapes=[pltpu.VMEM((nb, PAD_Q, ML_VW), F32), pltpu.VMEM((nb, 1, PAD_Q), F32),
                        pltpu.VMEM((nb, 1, LANE), F32)],
        compiler_params=_cparams(("arbitrary", "arbitrary")),
    )(proj3, proj3, proj3, proj3, proj3, ib_row, fb_row, gn_row, c0, n0, m0)
    return h.reshape(n, ML_VW), c1, n1, m1


def _layer_norm(y, g, b):
    mu = jnp.mean(y, axis=-1, keepdims=True)
    yc = y - mu
    var = jnp.mean(yc * yc, axis=-1, keepdims=True)
    return yc * lax.rsqrt(var + LN_EPS) * g + b


def _out_proj_kernel(ol_ref, of_ref, om_ref, x_ref, wuv_ref, wo_ref, g_ref, b_ref, o_ref, *, alpha):
    n_mla = MLA_HEADS * MLA_V
    t = jnp.dot(ol_ref[...].astype(BF16), wuv_ref[...], preferred_element_type=F32)
    mix = jnp.dot(t.astype(BF16), wo_ref[0:n_mla, :], preferred_element_type=F32)
    mix += jnp.dot(of_ref[...].astype(BF16), wo_ref[n_mla:n_mla + FOX_WIDTH, :], preferred_element_type=F32)
    mix += jnp.dot(om_ref[...].astype(BF16), wo_ref[n_mla + FOX_WIDTH:, :], preferred_element_type=F32)
    o_ref[...] = _layer_norm(alpha * x_ref[...] + mix, g_ref[...], b_ref[...])


def out_projection(olat, ofox, oml, x, wuv_bd, w_out, g, b, alpha, tm):
    n, d = x.shape
    const = lambda i: (0, 0)
    kern = functools.partial(_out_proj_kernel, alpha=alpha)
    return pl.pallas_call(
        kern,
        grid=(n // tm,),
        in_specs=[pl.BlockSpec((tm, olat.shape[1]), lambda i: (i, 0)),
                  pl.BlockSpec((tm, ofox.shape[1]), lambda i: (i, 0)),
                  pl.BlockSpec((tm, oml.shape[1]), lambda i: (i, 0)),
                  pl.BlockSpec((tm, d), lambda i: (i, 0)),
                  pl.BlockSpec(wuv_bd.shape, const), pl.BlockSpec(w_out.shape, const),
                  pl.BlockSpec((1, d), const), pl.BlockSpec((1, d), const)],
        out_specs=pl.BlockSpec((tm, d), lambda i: (i, 0)),
        out_shape=jax.ShapeDtypeStruct((n, d), F32),
        compiler_params=_cparams(("arbitrary",)),
    )(olat, ofox, oml, x, wuv_bd, w_out, g, b)


def _ffn_kernel(*refs, alpha, tm, fc, tiles_per_seq, short_seq):
    if short_seq:
        (x_ref, wg_ref, wu_ref, cw_ref, cb_ref, wd_ref, g_ref, b_ref, pe_ref, pg_ref, pp_ref, e1_ref, e2_ref,
         o_ref, gate_ref, acc_s, gext_s) = refs
    else:
        (x_ref, wg_ref, wu_ref, cw_ref, cb_ref, wd_ref, g_ref, b_ref, pe_ref, pg_ref, pp_ref, prev_ref,
         o_ref, cn_ref, acc_s, gext_s, carry_s) = refs
    i = pl.program_id(0)
    f = pl.program_id(1)
    nf = pl.num_programs(1)
    xb = x_ref[...].astype(BF16)
    gt = jnp.dot(xb, wg_ref[...], preferred_element_type=F32)
    up = jnp.dot(xb, wu_ref[...], preferred_element_type=F32)
    gext_s[SUBLANE:SUBLANE + tm, :] = gt
    if short_seq:
        gext_s[0:SUBLANE, :] = jnp.zeros((SUBLANE, fc), F32)
        gate_ref[...] = gt
        t_in = lax.broadcasted_iota(jnp.int32, (tm, 1), 0) % short_seq
        sh1 = jnp.where(t_in == 0, e1_ref[...], gext_s[SUBLANE - 1:SUBLANE - 1 + tm, :])
        sh2 = jnp.where(t_in < 2, e2_ref[...], gext_s[SUBLANE - 2:SUBLANE - 2 + tm, :])
    else:
        @pl.when(i % tiles_per_seq == 0)
        def _():
            gext_s[0:SUBLANE, :] = jnp.zeros((SUBLANE, fc), F32)
            gext_s[SUBLANE - 2:SUBLANE, :] = prev_ref[0]

        @pl.when(i % tiles_per_seq != 0)
        def _():
            gext_s[0:SUBLANE, :] = carry_s[f]

        carry_s[f] = gt[tm - SUBLANE:tm, :]
        cn_ref[0] = gt[tm - 2:tm, :]
        sh1 = gext_s[SUBLANE - 1:SUBLANE - 1 + tm, :]
        sh2 = gext_s[SUBLANE - 2:SUBLANE - 2 + tm, :]
    c = cb_ref[...] + cw_ref[0:1, :] * sh2 + cw_ref[1:2, :] * sh1 + cw_ref[2:3, :] * gt
    hh = 0.5 * c * (1.0 + lax.erf(c * (2.0 ** -0.5))) * up
    part = jnp.dot(hh.astype(BF16), wd_ref[...], preferred_element_type=F32)

    @pl.when(f == 0)
    def _():
        acc_s[...] = part

    @pl.when(f != 0)
    def _():
        acc_s[...] += part

    @pl.when(f == nf - 1)
    def _():
        y = _layer_norm(alpha * x_ref[...] + acc_s[...], g_ref[...], b_ref[...])
        gate = jax.nn.sigmoid(jnp.dot(y.astype(BF16), pg_ref[...], preferred_element_type=F32))
        o_ref[...] = y + gate * jnp.dot(pe_ref[...].astype(BF16), pp_ref[...], preferred_element_type=F32)


def _ffn_chunk(dff, wide):
    half = dff // 2
    if wide and dff % 2 == 0 and half % LANE == 0:
        return half
    return 256 if dff % 256 == 0 else dff


def conv_ffn(x, w_up, conv_w, conv_b, w_down, g, b, pe, ple_g, ple_p, alpha, tm, fc, *, seq_len, conv_prev=None,
             e1=None, e2=None):
    n, d = x.shape
    dff = w_down.shape[0]
    nf = dff // fc
    short_seq = seq_len if seq_len < tm else 0
    tiles_per_seq = max(seq_len // tm, 1)
    const = lambda i, f: (0, 0)
    in_specs = [pl.BlockSpec((tm, d), lambda i, f: (i, 0)),
                pl.BlockSpec((d, fc), lambda i, f: (0, f)),
                pl.BlockSpec((d, fc), lambda i, f: (0, nf + f)),
                pl.BlockSpec((CONV_W, fc), lambda i, f: (0, f)),
                pl.BlockSpec((1, fc), lambda i, f: (0, f)),
                pl.BlockSpec((fc, d), lambda i, f: (f, 0)),
                pl.BlockSpec((1, d), const), pl.BlockSpec((1, d), const),
                pl.BlockSpec((tm, pe.shape[1]), lambda i, f: (i, 0)),
                pl.BlockSpec(ple_g.shape, const), pl.BlockSpec(ple_p.shape, const)]
    args = [x, w_up, w_up, conv_w, conv_b, w_down, g, b, pe, ple_g, ple_p]
    scratch = [pltpu.VMEM((tm, d), F32), pltpu.VMEM((tm + SUBLANE, fc), F32)]
    if short_seq:
        in_specs += [pl.BlockSpec((tm, fc), lambda i, f: (i, f)), pl.BlockSpec((tm, fc), lambda i, f: (i, f))]
        args += [e1, e2]
        out_specs = [pl.BlockSpec((tm, d), lambda i, f: (i, 0)), pl.BlockSpec((tm, fc), lambda i, f: (i, f))]
        out_shape = [jax.ShapeDtypeStruct((n, d), F32), jax.ShapeDtypeStruct((n, dff), F32)]
    else:
        in_specs += [pl.BlockSpec((1, CONV_W - 1, fc), lambda i, f: (i // tiles_per_seq, 0, f))]
        args += [conv_prev]
        out_specs = [pl.BlockSpec((tm, d), lambda i, f: (i, 0)),
                     pl.BlockSpec((1, CONV_W - 1, fc), lambda i, f: (i, 0, f))]
        out_shape = [jax.ShapeDtypeStruct((n, d), F32),
                     jax.ShapeDtypeStruct((n // tm, CONV_W - 1, dff), F32)]
        scratch += [pltpu.VMEM((nf, SUBLANE, fc), F32)]
    kern = functools.partial(_ffn_kernel, alpha=alpha, tm=tm, fc=fc, tiles_per_seq=tiles_per_seq,
                             short_seq=short_seq)
    return pl.pallas_call(
        kern, grid=(n // tm, nf), in_specs=in_specs, out_specs=out_specs, out_shape=out_shape,
        scratch_shapes=scratch, compiler_params=_cparams(("arbitrary", "arbitrary")),
    )(*args)


def _sample_attn_kernel(pt_ref,
                        qc_ref, fq_ref, kcn_ref, fkn_ref, fvn_ref, fln_ref,
                        lat_hbm, rope_hbm, fk_hbm, fv_hbm, lf_hbm,
                        ol_ref, of_ref,
                        lat_b, rope_b, fk_b, fv_b, lf_b, sems,
                        ml_m, ml_l, ml_acc, fx_m, fx_l, fx_acc, fx_p,
                        *, layer, pb, nblk, nseq, t_new):
    b = pl.program_id(0)
    j = pl.program_id(1)
    step = b * nblk + j
    total = nseq * nblk
    slot = step % N_SLOTS
    page = lat_b.shape[2]

    def copies(bb, jj, sl):
        out = []
        for p in range(pb):
            pg = pt_ref[bb, jj * pb + p]
            cols = pl.ds(p * page, page)
            out.append(pltpu.make_async_copy(lat_hbm.at[pg, layer], lat_b.at[sl, p], sems.at[sl, 0]))
            out.append(pltpu.make_async_copy(rope_hbm.at[pg, layer], rope_b.at[sl, :, cols], sems.at[sl, 1]))
            out.append(pltpu.make_async_copy(fk_hbm.at[pg, layer], fk_b.at[sl, :, cols], sems.at[sl, 2]))
            out.append(pltpu.make_async_copy(fv_hbm.at[pg, layer], fv_b.at[sl, :, cols], sems.at[sl, 3]))
            out.append(pltpu.make_async_copy(lf_hbm.at[pg, layer], lf_b.at[sl, :, cols], sems.at[sl, 4]))
        return out

    def block_of(s):
        s = jnp.where(s >= total, s - total, s)
        return s // nblk, s % nblk

    @pl.when(step == 0)
    def _():
        for d in range(N_SLOTS - 1):
            bb, jj = block_of(step + d)
            for c in copies(bb, jj, d):
                c.start()

    rows_m = MLA_HEADS * t_new
    rows_f = FOX_HEADS * t_new
    keys = pb * page

    @pl.when(j == 0)
    def _():
        ml_m[...] = jnp.full_like(ml_m, -jnp.inf)
        ml_l[...] = jnp.zeros_like(ml_l)
        ml_acc[...] = jnp.zeros_like(ml_acc)
        fx_m[...] = jnp.full_like(fx_m, -jnp.inf)
        fx_l[...] = jnp.zeros_like(fx_l)
        fx_acc[...] = jnp.zeros_like(fx_acc)
        fx_p[...] = jnp.zeros_like(fx_p)

    qc = qc_ref[...].reshape(rows_m, 2 * LANE)
    lane_f = lax.broadcasted_iota(jnp.int32, (1, FOX_WIDTH), 1)
    fq = fq_ref[...] * FOX_QSCALE
    qf = jnp.concatenate([jnp.where((lane_f // FOX_DH) == h, fq, 0.0) for h in range(FOX_HEADS)],
                         axis=0).astype(BF16)
    r8 = lax.broadcasted_iota(jnp.int32, (SUBLANE, FOX_HEADS), 0)
    l8 = lax.broadcasted_iota(jnp.int32, (SUBLANE, FOX_HEADS), 1)
    sel = jnp.where(r8 == l8, 1.0, 0.0).astype(F32)

    def lane_cumsum(x):
        n = x.shape[1]
        idx = lax.broadcasted_iota(jnp.int32, x.shape, 1)
        sh = 1
        while sh < n:
            x = x + jnp.where(idx >= sh, pltpu.roll(x, sh, axis=1), 0.0)
            sh *= 2
        return x

    def expand_rows(x8):
        return jnp.concatenate([jnp.broadcast_to(x8[h:h + 1, :], (t_new, x8.shape[1])) for h in range(FOX_HEADS)],
                               axis=0)

    def online(m_s, l_s, acc_s, s, v, v_feature_major=False):
        m_old = m_s[...]
        m_new = jnp.maximum(m_old, jnp.max(s, axis=-1, keepdims=True))
        a = jnp.exp2(m_old - m_new)
        p = jnp.exp2(s - m_new)
        l_s[...] = a * l_s[...] + jnp.sum(p, axis=-1, keepdims=True)
        p = p.astype(BF16)
        pv = _nt_dot(p, v) if v_feature_major else jnp.dot(p, v, preferred_element_type=F32)
        acc_s[...] = a * acc_s[...] + pv
        m_s[...] = m_new

    def past_block(slot):
        latb = lat_b[slot].reshape(keys, MLA_KV_LORA).astype(BF16)
        rope_t = rope_b[slot].astype(BF16)
        s_m = (_nt_dot(qc[:, 0:LANE], latb)
               + jnp.dot(qc[:, LANE:LANE + MLA_ROPE], rope_t, preferred_element_type=F32))
        online(ml_m, ml_l, ml_acc, s_m, latb)
        fk_t = fk_b[slot].astype(BF16)
        fv_t = fv_b[slot].astype(BF16)
        pre = lane_cumsum(lf_b[slot]) + fx_p[...]
        fx_p[...] = pre[:, keys - 1:keys]
        s_f = jnp.dot(qf, fk_t, preferred_element_type=F32) - expand_rows(pre * LOG2E)
        online(fx_m, fx_l, fx_acc, s_f, fv_t, v_feature_major=True)

    for c in copies(b, j, slot):
        c.wait()
    past_block(slot)

    ahead = step + (N_SLOTS - 1)
    nb, nj = block_of(ahead)
    for c in copies(nb, nj, ahead % N_SLOTS):
        c.start()

    @pl.when(step == total - 1)
    def _():
        for d in range(1, N_SLOTS):
            bb, jj = block_of(step + d)
            for c in copies(bb, jj, (step + d) % N_SLOTS):
                c.wait()

    @pl.when(j == nblk - 1)
    def _():
        kcn = kcn_ref[...]
        tq_m = lax.broadcasted_iota(jnp.int32, (MLA_HEADS, t_new, t_new), 1).reshape(rows_m, t_new)
        ts_m = lax.broadcasted_iota(jnp.int32, (rows_m, t_new), 1)
        s_n = jnp.where(ts_m <= tq_m, _nt_dot(qc, kcn), -jnp.inf)
        online(ml_m, ml_l, ml_acc, s_n, kcn[:, 0:LANE])
        o_m = ml_acc[...] / ml_l[...]
        for h in range(MLA_HEADS):
            ol_ref[:, h * LANE:(h + 1) * LANE] = o_m[h * t_new:(h + 1) * t_new, :]

        f_total = expand_rows(fx_p[...])
        fx_m[...] = fx_m[...] + f_total * LOG2E
        fcum = lane_cumsum(_nt_dot(sel, fln_ref[:, G_FOX:G_FOX + FOX_HEADS], precision=HI))
        tq_f = lax.broadcasted_iota(jnp.int32, (FOX_HEADS, t_new, t_new), 1).reshape(rows_f, t_new)
        ts_f = lax.broadcasted_iota(jnp.int32, (rows_f, t_new), 1)
        s_n = jnp.where(ts_f <= tq_f, _nt_dot(qf, fkn_ref[...]) - expand_rows(fcum * LOG2E), -jnp.inf)
        online(fx_m, fx_l, fx_acc, s_n, fvn_ref[...])
        o_f = fx_acc[...] / fx_l[...]
        out = jnp.zeros((t_new, FOX_WIDTH), F32)
        for h in range(FOX_HEADS):
            out = jnp.where((lane_f // FOX_DH) == h, o_f[h * t_new:(h + 1) * t_new, :], out)
        of_ref[...] = out


def sample_attention(page_table, qcat, proj, kcat, fkb, fvb, flog, caches, layer, nseq, t_new, pb):
    lat_c, rope_c, fk_c, fv_c, lf_c = caches
    n_pages = page_table.shape[1]
    page = lat_c.shape[2]
    nblk = n_pages // pb
    assert n_pages % pb == 0 and nseq * nblk >= N_SLOTS
    keys = pb * page
    rope_c = jnp.transpose(rope_c, (0, 1, 3, 2))
    fk_c = jnp.transpose(fk_c, (0, 1, 3, 4, 2)).reshape(fk_c.shape[0], fk_c.shape[1], FOX_WIDTH, page)
    fv_c = jnp.transpose(fv_c, (0, 1, 3, 4, 2)).reshape(fv_c.shape[0], fv_c.shape[1], FOX_WIDTH, page)
    lf_c = jnp.pad(jnp.transpose(lf_c, (0, 1, 3, 2)), ((0, 0), (0, 0), (0, SUBLANE - FOX_HEADS), (0, 0)))
    rows_m = MLA_HEADS * t_new
    rows_f = FOX_HEADS * t_new
    kern = functools.partial(_sample_attn_kernel, layer=layer, pb=pb, nblk=nblk, nseq=nseq, t_new=t_new)
    any_spec = pl.BlockSpec(memory_space=pl.ANY)
    grid_spec = pltpu.PrefetchScalarGridSpec(
        num_scalar_prefetch=1,
        grid=(nseq, nblk),
        in_specs=[pl.BlockSpec((MLA_HEADS, t_new, 2 * LANE), lambda b, j, pt: (0, b, 0)),
                  pl.BlockSpec((t_new, FOX_WIDTH), lambda b, j, pt: (b, C_FQ // FOX_WIDTH)),
                  pl.BlockSpec((t_new, 2 * LANE), lambda b, j, pt: (b, 0)),
                  pl.BlockSpec((t_new, FOX_WIDTH), lambda b, j, pt: (b, 0)),
                  pl.BlockSpec((t_new, FOX_WIDTH), lambda b, j, pt: (b, 0)),
                  pl.BlockSpec((t_new, LANE), lambda b, j, pt: (b, 0)),
                  any_spec, any_spec, any_spec, any_spec, any_spec],
        out_specs=[pl.BlockSpec((t_new, MLA_HEADS * LANE), lambda b, j, pt: (b, 0)),
                   pl.BlockSpec((t_new, FOX_WIDTH), lambda b, j, pt: (b, 0))],
        scratch_shapes=[pltpu.VMEM((N_SLOTS, pb, page, MLA_KV_LORA), F32),
                        pltpu.VMEM((N_SLOTS, MLA_ROPE, keys), F32),
                        pltpu.VMEM((N_SLOTS, FOX_WIDTH, keys), F32),
                        pltpu.VMEM((N_SLOTS, FOX_WIDTH, keys), F32),
                        pltpu.VMEM((N_SLOTS, SUBLANE, keys), F32),
                        pltpu.SemaphoreType.DMA((N_SLOTS, 5)),
                        pltpu.VMEM((rows_m, 1), F32), pltpu.VMEM((rows_m, 1), F32),
                        pltpu.VMEM((rows_m, LANE), F32),
                        pltpu.VMEM((rows_f, 1), F32), pltpu.VMEM((rows_f, 1), F32),
                        pltpu.VMEM((rows_f, FOX_WIDTH), F32),
                        pltpu.VMEM((SUBLANE, 1), F32)],
    )
    n = nseq * t_new
    return pl.pallas_call(
        kern,
        grid_spec=grid_spec,
        out_shape=[jax.ShapeDtypeStruct((n, MLA_HEADS * LANE), F32),
                   jax.ShapeDtypeStruct((n, FOX_WIDTH), F32)],
        compiler_params=_cparams(("arbitrary", "arbitrary")),
    )(page_table, qcat, proj, kcat, fkb, fvb, flog, lat_c, rope_c, fk_c, fv_c, lf_c)


def _pad_cols(w, width):
    return jnp.pad(w, ((0, 0), (0, width - w.shape[1])))


def _prep_layer_weights(w_in, mla_w_uq, mla_w_uk, mla_w_uv):
    d = w_in.shape[0]
    splits = (MLA_Q_LORA, MLA_KV_LORA, MLA_ROPE, FOX_WIDTH, FOX_WIDTH, FOX_WIDTH, FOX_HEADS,
              ML_QW, ML_QW, ML_VW, ML_HEADS, ML_HEADS, ML_VW)
    offs = [0]
    for s in splits:
        offs.append(offs[-1] + s)
    (cq, ckv, kr, fq, fk, fv, ff, mq, mk, mv, mi, mf, mo) = [w_in[:, offs[i]:offs[i + 1]] for i in range(len(splits))]
    half = MLA_ROPE // 2
    swap = lambda w: jnp.concatenate([w[..., half:], w[..., :half]], axis=-1)
    kr_blk = _pad_cols(jnp.concatenate([kr, swap(kr)], axis=1), LANE)
    gate_blk = jnp.concatenate([_pad_cols(ff, G_MI - G_FOX), _pad_cols(mi, G_MF - G_MI), _pad_cols(mf, LANE - G_MF)],
                               axis=1)
    w_ext = jnp.concatenate([cq, ckv, kr_blk, fq, fk, fv, _pad_cols(mq, PAD_Q), _pad_cols(mk, PAD_Q), gate_blk,
                             mv, mo], axis=1).astype(BF16)
    assert w_ext.shape == (d, D_EXT)
    uq = mla_w_uq.reshape(MLA_Q_LORA, MLA_HEADS, MLA_NOPE + MLA_ROPE)
    wn = uq[:, :, :MLA_NOPE].reshape(MLA_Q_LORA, MLA_HEADS * MLA_NOPE)
    rope = uq[:, :, MLA_NOPE:]
    pad_r = lambda w: jnp.pad(w, ((0, 0), (0, 0), (0, LANE - MLA_ROPE))).reshape(MLA_Q_LORA, MLA_HEADS * LANE)
    wr = pad_r(rope)
    wrs = pad_r(swap(rope))
    eye = jnp.eye(MLA_HEADS, dtype=F32)
    uk = jnp.transpose(mla_w_uk, (1, 2, 0))
    wuk_bd = (uk[:, :, None, :] * eye[:, None, :, None]).reshape(MLA_HEADS * MLA_NOPE, MLA_HEADS * MLA_KV_LORA)
    uv = jnp.transpose(mla_w_uv, (1, 0, 2))
    wuv_bd = (uv[:, :, None, :] * eye[:, None, :, None]).reshape(MLA_HEADS * MLA_KV_LORA, MLA_HEADS * MLA_V)
    return w_ext, wn.astype(BF16), wr.astype(BF16), wrs.astype(BF16), wuk_bd.astype(BF16), wuv_bd.astype(BF16)


def _rope_tables(pos, reps):
    inv_freq = 1.0 / (ROPE_THETA ** (jnp.arange(0, MLA_ROPE, 2, dtype=F32) / MLA_ROPE))
    ang = pos.astype(F32)[:, None] * inv_freq[None, :]
    cos, sin = jnp.cos(ang), jnp.sin(ang)
    cos_t = _pad_cols(jnp.concatenate([cos, cos], axis=1), LANE)
    sin_t = _pad_cols(jnp.concatenate([-sin, sin], axis=1), LANE)
    return jnp.tile(cos_t, (reps, 1)), jnp.tile(sin_t, (reps, 1))


def _gate_row(vec, off):
    return jnp.zeros((1, LANE), F32).at[0, off:off + vec.shape[0]].set(vec)


def _state_to_blockdiag(c, n, m):
    bsz = c.shape[0]
    eye = jnp.eye(ML_HEADS, dtype=F32)
    cbd = (c[:, :, :, None, :] * eye[None, :, None, :, None]).reshape(bsz, ML_QW, ML_VW)
    cbd = jnp.pad(cbd, ((0, 0), (0, PAD_Q - ML_QW), (0, 0)))
    nrow = jnp.pad(n.reshape(bsz, 1, ML_QW), ((0, 0), (0, 0), (0, PAD_Q - ML_QW)))
    mrow = jnp.pad(m.reshape(bsz, 1, ML_HEADS), ((0, 0), (0, 0), (0, LANE - ML_HEADS)))
    return cbd, nrow, mrow


def _blockdiag_to_state(cbd, nrow, mrow):
    bsz = cbd.shape[0]
    c5 = cbd[:, :ML_QW, :].reshape(bsz, ML_HEADS, ML_DK, ML_HEADS, ML_DV)
    c = jnp.stack([c5[:, h, :, h, :] for h in range(ML_HEADS)], axis=1)
    n = nrow[:, 0, :ML_QW].reshape(bsz, ML_HEADS, ML_DK)
    m = mrow[:, 0, :ML_HEADS]
    return c, n, m


def _run_group(x, pe, lw, *, batch, seq, pos, alpha, tm, past):
    n = x.shape[0]
    w_ext, wn, wr, wrs, wuk_bd, wuv_bd = lw['prepped']
    proj = in_projection(x, w_ext, tm)
    reps = 1 if past is None else n // seq
    cos_t, sin_t = _rope_tables(pos, reps)
    lat, kpe, kcat, qcat = mla_prep(proj, cos_t, sin_t, lw['mla_q_norm'][None, :], lw['mla_kv_norm'][None, :],
                                    wn, wr, wrs, wuk_bd, tm)
    fox_bias = _gate_row(lw['fox_f_bias'], G_FOX)
    kbf, vbf, flog, fcum, kf32, vf32 = fox_prep(proj, fox_bias, seq, tm)
    ib_row = _gate_row(lw['mlstm_i_bias'], G_MI)
    fb_row = _gate_row(lw['mlstm_f_bias'], G_MF)
    gn_row = lw['mlstm_norm'].reshape(1, ML_VW)
    if past is None:
        olat = mla_flash(qcat, kcat, batch, seq, tq=256, tk=512)
        fcum_t = jnp.transpose(fcum.reshape(batch, seq, LANE)[:, :, :SUBLANE], (0, 2, 1))
        ofox = fox_flash(proj, kbf, vbf, fcum_t, batch, seq, tq=256, tk=512)
        c0 = jnp.zeros((batch, PAD_Q, ML_VW), F32)
        n0 = jnp.zeros((batch, 1, PAD_Q), F32)
        m0 = jnp.zeros((batch, 1, LANE), F32)
        chunk = ML_CHUNK if seq % ML_CHUNK == 0 else seq
    else:
        olat, ofox = sample_attention(past['page_table'], qcat, proj, kcat, kbf, vbf, flog, past['caches'],
                                      past['layer'], batch, seq, pb=min(32, past['page_table'].shape[1] // 2))
        c0, n0, m0 = _state_to_blockdiag(past['c'], past['n'], past['m'])
        chunk = ML_CHUNK if seq % ML_CHUNK == 0 else seq
    seqs_per_step = next(d for d in (4, 2, 1) if batch % d == 0)
    oml, c1, n1, m1 = mlstm(proj, ib_row, fb_row, gn_row, c0, n0, m0, batch, seq, chunk, seqs_per_step)
    c1, n1, m1 = _blockdiag_to_state(c1, n1, m1)
    x1 = out_projection(olat, ofox, oml, x, wuv_bd, lw['w_out'], lw['ln1_g'][None, :], lw['ln1_b'][None, :], alpha, tm)
    dff = lw['ffn_w_down'].shape[0]
    ffn_args = (x1, lw['ffn_w_up'], lw['ffn_conv_w'], lw['ffn_conv_b'][None, :], lw['ffn_w_down'],
                lw['ln2_g'][None, :], lw['ln2_b'][None, :], pe, lw['ple_w_gate'], lw['ple_w_proj'], alpha, tm,
                _ffn_chunk(dff, wide=past is None))
    if past is None:
        x2, tails = conv_ffn(*ffn_args, seq_len=seq, conv_prev=jnp.zeros((batch, CONV_W - 1, dff), F32))
        conv_new = tails.reshape(batch, seq // tm, CONV_W - 1, dff)[:, -1]
    else:
        prev = past['conv']
        e1 = jnp.pad(prev[:, 1:2], ((0, 0), (0, seq - 1), (0, 0))).reshape(n, dff)
        e2 = jnp.pad(prev, ((0, 0), (0, seq - 2), (0, 0))).reshape(n, dff)
        x2, gate = conv_ffn(*ffn_args, seq_len=seq, e1=e1, e2=e2)
        conv_new = gate.reshape(batch, seq, dff)[:, seq - (CONV_W - 1):]
    d_lat = lat.reshape(batch, seq, MLA_KV_LORA)
    d_rope = kpe[:, :MLA_ROPE].reshape(batch, seq, MLA_ROPE)
    d_fk = kf32.reshape(batch, seq, FOX_HEADS, FOX_DH)
    d_fv = vf32.reshape(batch, seq, FOX_HEADS, FOX_DH)
    d_lf = flog[:, G_FOX:G_FOX + FOX_HEADS].reshape(batch, seq, FOX_HEADS)
    return x2, (d_lat, d_rope, d_fk, d_fv, d_lf, c1, n1, m1, conv_new)


def kernel(x_prompt, x_sample, cache_mla_latent, cache_mla_rope, cache_fox_k, cache_fox_v, cache_fox_logf,
           state_mlstm_c, state_mlstm_n, state_mlstm_m, state_ffn_conv, page_table, p_prompt, p_sample,
           w_in, mla_q_norm, mla_w_uq, mla_kv_norm, mla_w_uk, mla_w_uv, fox_f_bias, mlstm_i_bias, mlstm_f_bias,
           mlstm_norm, w_out, ln1_g, ln1_b, ffn_w_up, ffn_conv_w, ffn_conv_b, ffn_w_down, ln2_g, ln2_b,
           ple_w_gate, ple_w_proj):
    depth = w_in.shape[0]
    bp, sp, d = x_prompt.shape
    bs, ss, _ = x_sample.shape
    past_len = page_table.shape[1] * cache_mla_latent.shape[2]
    alpha = (2 * depth) ** 0.25
    pos_p = jnp.arange(sp)
    pos_s = past_len + jnp.arange(ss)
    xp = x_prompt.reshape(bp * sp, d)
    xs = x_sample.reshape(bs * ss, d)
    caches = (cache_mla_latent, cache_mla_rope, cache_fox_k, cache_fox_v, cache_fox_logf)
    per_p, per_s = [], []
    for i in range(depth):
        lw = {'prepped': _prep_layer_weights(w_in[i], mla_w_uq[i], mla_w_uk[i], mla_w_uv[i]),
              'mla_q_norm': mla_q_norm[i], 'mla_kv_norm': mla_kv_norm[i], 'fox_f_bias': fox_f_bias[i],
              'mlstm_i_bias': mlstm_i_bias[i], 'mlstm_f_bias': mlstm_f_bias[i], 'mlstm_norm': mlstm_norm[i],
              'w_out': w_out[i].astype(BF16), 'ln1_g': ln1_g[i], 'ln1_b': ln1_b[i],
              'ffn_w_up': ffn_w_up[i].astype(BF16), 'ffn_conv_w': ffn_conv_w[i], 'ffn_conv_b': ffn_conv_b[i],
              'ffn_w_down': ffn_w_down[i].astype(BF16), 'ln2_g': ln2_g[i], 'ln2_b': ln2_b[i],
              'ple_w_gate': ple_w_gate[i].astype(BF16), 'ple_w_proj': ple_w_proj[i].astype(BF16)}
        xp, st_p = _run_group(xp, p_prompt[i].reshape(bp * sp, -1), lw, batch=bp, seq=sp, pos=pos_p, alpha=alpha,
                              tm=min(512, bp * sp), past=None)
        per_p.append(st_p)
        past = {'page_table': page_table, 'caches': caches, 'layer': i, 'c': state_mlstm_c[i],
                'n': state_mlstm_n[i], 'm': state_mlstm_m[i], 'conv': state_ffn_conv[i]}
        xs, st_s = _run_group(xs, p_sample[i].reshape(bs * ss, -1), lw, batch=bs, seq=ss, pos=pos_s, alpha=alpha,
                              tm=min(512, bs * ss), past=past)
        per_s.append(st_s)

    def stack(per_layer):
        cols = list(zip(*per_layer))
        return [jnp.stack(c, axis=1 if k < 5 else 0) for k, c in enumerate(cols)]

    return tuple([xp.reshape(bp, sp, d), xs.reshape(bs, ss, d)] + stack(per_p) + stack(per_s))
```
